```python
import math
import jax, jax.numpy as jnp
from jax import lax
import numpy as np

D_MODEL = 1024
BATCH = 8
SEQ = 2048
DEPTH = 2
DEC_BATCH = 16
DEC_SEQ = 2048
PAST_LEN = 128

HEAD_DIM = 64
A_HEADS = D_MODEL // 128
A_WIDTH = A_HEADS * HEAD_DIM
DILATED_PATTERNS = ((128, 1), (512, 4), (2048, 16))
R_HEADS = D_MODEL // 256
R_KEY_DIM = HEAD_DIM
R_VAL_DIM = 2 * HEAD_DIM
R_QK_WIDTH = R_HEADS * R_KEY_DIM
R_WIDTH = R_HEADS * R_VAL_DIM
MIX_WIDTH = A_WIDTH + R_WIDTH
IN_SPLITS = (A_WIDTH, A_WIDTH, A_WIDTH, A_WIDTH, R_QK_WIDTH, R_QK_WIDTH, R_WIDTH, R_WIDTH)
IN_WIDTH = sum(IN_SPLITS)
CHUNK = 128
ROPE_THETA = 10000.0
EPS = 1e-6
NEG = -1e30

kernel_name = "hybrid_dilated_attn_retention_encoder"


def rmsnorm(x, g):
    xf = x.astype(jnp.float32)
    y = xf * lax.rsqrt(jnp.mean(xf * xf, axis=-1, keepdims=True) + EPS)
    return (y * g.astype(jnp.float32)).astype(x.dtype)


def rope(x):
    S, dh = x.shape[1], x.shape[-1]
    inv = ROPE_THETA ** (-jnp.arange(0, dh, 2, dtype=jnp.float32) / dh)
    ang = jnp.arange(S, dtype=jnp.float32)[:, None] * inv[None, :]
    cos = jnp.cos(ang)[None, :, None, :].astype(x.dtype)
    sin = jnp.sin(ang)[None, :, None, :].astype(x.dtype)
    x1, x2 = x[..., : dh // 2], x[..., dh // 2:]
    return jnp.concatenate([x1 * cos - x2 * sin, x2 * cos + x1 * sin], axis=-1)


def dilated_attention(q, k, v, window, dilation):
    B, S, H, dh = q.shape
    R = window // (2 * dilation)
    blk = R
    L = S // dilation
    nb = -(-L // blk)
    Lp = nb * blk

    def to_classes(t):
        return t.reshape(B, L, dilation, H, dh).transpose(0, 2, 1, 3, 4)

    qc = jnp.pad(to_classes(q), ((0, 0), (0, 0), (0, Lp - L), (0, 0), (0, 0)))
    qc = qc.reshape(B, dilation, nb, blk, H, dh)

    def windows(t):
        tp = jnp.pad(to_classes(t), ((0, 0), (0, 0), (blk, Lp - L + blk), (0, 0), (0, 0)))
        tp = tp.reshape(B, dilation, nb + 2, blk, H, dh)
        return jnp.concatenate([tp[:, :, :-2], tp[:, :, 1:-1], tp[:, :, 2:]], axis=3)

    kw, vw = windows(k), windows(v)
    s = jnp.einsum('bdnqhe,bdnkhe->bdnhqk', qc, kw).astype(jnp.float32)
    l_q = jnp.arange(nb)[:, None, None] * blk + jnp.arange(blk)[None, :, None]
    l_k = (jnp.arange(nb)[:, None, None] - 1) * blk + jnp.arange(3 * blk)[None, None, :]
    valid = (l_k >= 0) & (l_k < L) & (jnp.abs(l_q - l_k) <= R)
    s = jnp.where(valid[None, None, :, None], s, NEG)
    lse = jax.nn.logsumexp(s, axis=-1)
    p = jnp.exp(s - lse[..., None]).astype(v.dtype)
    o = jnp.einsum('bdnhqk,bdnkhe->bdnqhe', p, vw)
    o = o.reshape(B, dilation, Lp, H, dh)[:, :, :L].transpose(0, 2, 1, 3, 4).reshape(B, S, H, dh)
    lse = lse.transpose(0, 1, 2, 4, 3).reshape(B, dilation, Lp, H)[:, :, :L]
    lse = lse.transpose(0, 2, 1, 3).reshape(B, S, H)
    return o, lse


def retention_direction(q, k, v, log_g, strict):
    B, S, H, dk = q.shape
    dv = v.shape[-1]
    nC = S // CHUNK
    q = q.reshape(B, nC, CHUNK, H, dk)
    k = k.reshape(B, nC, CHUNK, H, dk)
    v = v.reshape(B, nC, CHUNK, H, dv)
    t = jnp.arange(CHUNK, dtype=jnp.float32)
    diff = t[:, None] - t[None, :]
    mask = (diff > 0) if strict else (diff >= 0)
    D = jnp.where(mask[None], jnp.exp(jnp.where(mask, diff, 0.0)[None] * log_g[:, None, None]), 0.0)
    inner = jnp.einsum('bnthe,bnshe->bnhts', q, k) * D[None, None]
    y = jnp.einsum('bnhts,bnshf->bnthf', inner, v)
    kdec = k * jnp.exp((CHUNK - 1 - t)[:, None] * log_g[None, :])[None, None, :, :, None]
    chunk_kv = jnp.einsum('bnshe,bnshf->nbhef', kdec, v)
    g_chunk = jnp.exp(CHUNK * log_g)[None, :, None, None]

    def step(state, kv):
        return g_chunk * state + kv, state

    _, state_prev = lax.scan(step, jnp.zeros((B, H, dk, dv), jnp.float32), chunk_kv)
    qdec = q * jnp.exp((t + 1.0)[:, None] * log_g[None, :])[None, None, :, :, None]
    y = y + jnp.einsum('bnthe,nbhef->bnthf', qdec, state_prev)
    return y.reshape(B, S, H, dv)


def encoder_layer(x, c, g_norm, w_ada, b_ada, w_in, w_out, decay_fwd, decay_bwd):
    B, S, _ = x.shape
    mod = jax.nn.silu(c) @ w_ada + b_ada
    shift, scale, gate = jnp.split(mod, 3, axis=-1)
    h = rmsnorm(x, g_norm) * (1.0 + scale[:, None, :]) + shift[:, None, :]
    proj = h @ w_in
    idx = list(np.cumsum(IN_SPLITS)[:-1])
    qa, ka, va, ga, qb, kb, vb, gb = jnp.split(proj, idx, axis=-1)

    qa = rope(qa.reshape(B, S, A_HEADS, HEAD_DIM)) * (HEAD_DIM ** -0.5)
    ka = rope(ka.reshape(B, S, A_HEADS, HEAD_DIM))
    va = va.reshape(B, S, A_HEADS, HEAD_DIM)
    outs, lses = [], []
    for window, dilation in DILATED_PATTERNS:
        o_p, lse_p = dilated_attention(qa, ka, va, window, dilation)
        outs.append(o_p)
        lses.append(lse_p)
    wts = jax.nn.softmax(jnp.stack(lses, axis=0), axis=0).astype(x.dtype)
    oa = jnp.einsum('pbsh,pbshe->bshe', wts, jnp.stack(outs, axis=0))
    ya = oa.reshape(B, S, A_WIDTH) * jax.nn.silu(ga)

    qf = rope(qb.reshape(B, S, R_HEADS, R_KEY_DIM)).astype(jnp.float32)
    kf = rope(kb.reshape(B, S, R_HEADS, R_KEY_DIM)).astype(jnp.float32) * (R_KEY_DIM ** -0.5)
    vf = vb.reshape(B, S, R_HEADS, R_VAL_DIM).astype(jnp.float32)
    lg_f = jax.nn.log_sigmoid(decay_fwd.astype(jnp.float32))
    lg_b = jax.nn.log_sigmoid(decay_bwd.astype(jnp.float32))
    yf = retention_direction(qf, kf, vf, lg_f, False)
    yb = retention_direction(qf[:, ::-1], kf[:, ::-1], vf[:, ::-1], lg_b, True)[:, ::-1]
    r = yf + yb
    r = r * lax.rsqrt(jnp.mean(r * r, axis=-1, keepdims=True) + EPS)
    yr = r.astype(x.dtype).reshape(B, S, R_WIDTH) * jax.nn.silu(gb)

    out = jnp.concatenate([ya, yr], axis=-1) @ w_out
    return x + gate[:, None, :] * out


def trunk(x, c, g_norm, w_ada, b_ada, w_in, w_out, decay_fwd, decay_bwd, g_final):
    for l in range(DEPTH):
        x = encoder_layer(x, c, g_norm[l], w_ada[l], b_ada[l], w_in[l], w_out[l],
                          decay_fwd[l], decay_bwd[l])
    return rmsnorm(x, g_final)


def setup_inputs(seed: int = 0) -> dict:
    key = jax.random.key(seed)
    ks = jax.random.split(key, 14)
    f32 = jnp.float32
    base = 1.0 - 2.0 ** (-5.0 - jnp.arange(R_HEADS, dtype=f32))
    base_logit = jnp.log(base / (1.0 - base))
    return {
        "x_prompt": jax.random.normal(ks[0], (BATCH, SEQ, D_MODEL), f32),
        "x_sample": jax.random.normal(ks[1], (DEC_BATCH, DEC_SEQ, D_MODEL), f32),
        "c_prompt": jax.random.normal(ks[2], (BATCH, D_MODEL), f32),
        "c_sample": jax.random.normal(ks[3], (DEC_BATCH, D_MODEL), f32),
        "g_norm": 1.0 + 0.02 * jax.random.normal(ks[4], (DEPTH, D_MODEL), f32),
        "w_ada": 0.5 * D_MODEL ** -0.5 * jax.random.normal(ks[5], (DEPTH, D_MODEL, 3 * D_MODEL), f32),
        "b_ada": 0.02 * jax.random.normal(ks[6], (DEPTH, 3 * D_MODEL), f32),
        "w_in": D_MODEL ** -0.5 * jax.random.normal(ks[7], (DEPTH, D_MODEL, IN_WIDTH), f32),
        "w_out": MIX_WIDTH ** -0.5 * jax.random.normal(ks[8], (DEPTH, MIX_WIDTH, D_MODEL), f32),
        "decay_fwd": base_logit[None] + 0.1 * jax.random.normal(ks[9], (DEPTH, R_HEADS), f32),
        "decay_bwd": base_logit[None] + 0.1 * jax.random.normal(ks[10], (DEPTH, R_HEADS), f32),
        "g_final": 1.0 + 0.02 * jax.random.normal(ks[11], (D_MODEL,), f32),
    }


def reference(x_prompt, x_sample, c_prompt, c_sample, g_norm, w_ada, b_ada, w_in, w_out,
              decay_fwd, decay_bwd, g_final):
    y_prompt = trunk(x_prompt, c_prompt, g_norm, w_ada, b_ada, w_in, w_out,
                     decay_fwd, decay_bwd, g_final)
    y_sample = trunk(x_sample, c_sample, g_norm, w_ada, b_ada, w_in, w_out,
                     decay_fwd, decay_bwd, g_final)
    return (y_prompt, y_sample)
```

```python
import functools

import jax
import jax.numpy as jnp
from jax import lax
from jax.experimental import pallas as pl
from jax.experimental.pallas import tpu as pltpu

F32 = jnp.float32
BF16 = jnp.bfloat16

D_MODEL = 1024
SEQ = 2048
DEPTH = 2
HEAD_DIM = 64
A_WIDTH = 512
R_HEADS = 4
R_WIDTH = 512
MIX_WIDTH = 1024
IN_WIDTH = 3584
CHUNK = 128
N_CHUNKS = SEQ // CHUNK
ROPE_THETA = 10000.0
EPS = 1e-6
NEG = -1e30
RADIUS = 64
DILATIONS = (1, 4, 16)

LANES = 128
TQ = 128
ROW_TILE = 512
COL_CHUNK = 512
VMEM_LIMIT = 48 * 1024 * 1024

COL_QA, COL_KA, COL_VA, COL_GA = 0, 512, 1024, 1536
COL_QB, COL_KB, COL_VB, COL_GB = 2048, 2304, 2560, 3072


def _sigmoid(x):
    return 1.0 / (1.0 + jnp.exp(-x))


def _mod_kernel(c_ref, w_ref, b_ref, o_ref):
    c = c_ref[...]
    a = c * _sigmoid(c)
    o_ref[0] = jnp.dot(a, w_ref[0], preferred_element_type=F32,
                       precision=lax.Precision.HIGHEST) + b_ref[0]


def _modulation(c, w_ada, b_ada):
    B = c.shape[0]
    nt = 512
    return pl.pallas_call(
        _mod_kernel,
        grid=(DEPTH, 3 * D_MODEL // nt),
        in_specs=[
            pl.BlockSpec((B, D_MODEL), lambda l, j: (0, 0)),
            pl.BlockSpec((1, D_MODEL, nt), lambda l, j: (l, 0, j)),
            pl.BlockSpec((1, 1, nt), lambda l, j: (l, 0, j)),
        ],
        out_specs=pl.BlockSpec((1, B, nt), lambda l, j: (l, 0, j)),
        out_shape=jax.ShapeDtypeStruct((DEPTH, B, 3 * D_MODEL), F32),
        name="adaln_mod",
    )(c, w_ada, b_ada.reshape(DEPTH, 1, 3 * D_MODEL))


def _inproj_kernel(x_ref, shift_ref, scale_ref, g_ref, w_ref, cos_ref, sin_ref, o_ref):
    x = x_ref[0]
    ms = jnp.mean(x * x, axis=-1, keepdims=True)
    h = x * lax.rsqrt(ms + EPS) * g_ref[...]
    h = h * (1.0 + scale_ref[0]) + shift_ref[0]
    hb = h.astype(BF16)
    cos = cos_ref[...]
    sin = sin_ref[...]
    lane = lax.broadcasted_iota(jnp.int32, cos.shape, 1)
    first_half = (lane & 32) == 0
    for c in range(IN_WIDTH // COL_CHUNK):
        acc = jnp.dot(hb, w_ref[:, c * COL_CHUNK:(c + 1) * COL_CHUNK],
                      preferred_element_type=F32)
        for j in range(COL_CHUNK // LANES):
            col = c * COL_CHUNK + j * LANES
            a = acc[:, j * LANES:(j + 1) * LANES]
            if col < COL_VA or COL_QB <= col < COL_VB:
                partner = jnp.where(first_half, pltpu.roll(a, 96, 1), pltpu.roll(a, 32, 1))
                a = a * cos + partner * sin
            if col < COL_KA or COL_KB <= col < COL_VB:
                a = a * (HEAD_DIM ** -0.5)
            if COL_GA <= col < COL_QB or col >= COL_GB:
                a = a * _sigmoid(a)
            o_ref[0, :, col:col + LANES] = a.astype(BF16)


def _in_projection(x, mod3, g, w_bf16, cos_t, sin_t):
    B = x.shape[0]
    tm = ROW_TILE
    nt = SEQ // tm
    return pl.pallas_call(
        _inproj_kernel,
        grid=(B, nt),
        in_specs=[
            pl.BlockSpec((1, tm, D_MODEL), lambda b, i: (b, i, 0)),
            pl.BlockSpec((1, 1, D_MODEL), lambda b, i: (b, 0, 0)),
            pl.BlockSpec((1, 1, D_MODEL), lambda b, i: (b, 0, 1)),
            pl.BlockSpec((1, D_MODEL), lambda b, i: (0, 0)),
            pl.BlockSpec((D_MODEL, IN_WIDTH), lambda b, i: (0, 0)),
            pl.BlockSpec((tm, LANES), lambda b, i: (i, 0)),
            pl.BlockSpec((tm, LANES), lambda b, i: (i, 0)),
        ],
        out_specs=pl.BlockSpec((1, tm, IN_WIDTH), lambda b, i: (b, i, 0)),
        out_shape=jax.ShapeDtypeStruct((B, SEQ, IN_WIDTH), BF16),
        compiler_params=pltpu.CompilerParams(
            dimension_semantics=("parallel", "parallel"), vmem_limit_bytes=VMEM_LIMIT),
        name="in_proj",
    )(x, mod3, mod3, g, w_bf16, cos_t, sin_t)


def _band_bias(off, width):
    r = lax.broadcasted_iota(jnp.int32, (TQ, width), 0)
    c = lax.broadcasted_iota(jnp.int32, (TQ, width), 1)
    d = r + off - c
    return jnp.where((d <= RADIUS) & (d >= -RADIUS), 0.0, NEG).astype(F32)


def _head_ones(width):
    r = lax.broadcasted_iota(jnp.int32, (2 * width, LANES), 0)
    c = lax.broadcasted_iota(jnp.int32, (2 * width, LANES), 1)
    return jnp.where((r < width) == (c < HEAD_DIM), 1.0, 0.0).astype(BF16)


def _attn_kernel(q_ref, k_ref, v_ref, g_ref, o_ref,
                 qf, kf, vf, qc, kc, vc, acc_s, m_s, l_s):
    qn, kn, vn = q_ref.at[0], k_ref.at[0], v_ref.at[0]

    def attend(qr, kr, vr, q0, w0, width, bias, ones, store):
        q = qr[pl.ds(q0, TQ), :]
        head0 = lax.broadcasted_iota(jnp.int32, q.shape, 1) < HEAD_DIM
        zq = jnp.zeros_like(q)
        qs = jnp.concatenate([jnp.where(head0, q, zq), jnp.where(head0, zq, q)], axis=0)
        kw = kr[pl.ds(w0, width), :]
        s = lax.dot_general(qs, kw, (((1,), (1,)), ((), ())),
                            preferred_element_type=F32)
        s = s + jnp.concatenate([bias, bias], axis=0)
        m = jnp.max(s, axis=-1, keepdims=True)
        p = jnp.exp(s - m).astype(BF16)
        pc = jnp.concatenate([p[:TQ], p[TQ:]], axis=1)
        vw = vr[pl.ds(w0, width), :]
        vhead0 = lax.broadcasted_iota(jnp.int32, vw.shape, 1) < HEAD_DIM
        zv = jnp.zeros_like(vw)
        vs = jnp.concatenate([jnp.where(vhead0, vw, zv), jnp.where(vhead0, zv, vw)], axis=0)
        r = jnp.dot(pc, jnp.concatenate([vs, ones], axis=1),
                    preferred_element_type=F32)
        h0f = lax.broadcasted_iota(jnp.int32, (TQ, LANES), 1) < HEAD_DIM
        mexp = jnp.where(h0f, m[:TQ], m[TQ:])
        store(r[:, :LANES], mexp, r[:, LANES:])

    def store_plane(p, idx):
        def store(acc, mexp, lsum):
            acc_s[p, idx, :] = acc
            m_s[p, idx, :] = mexp
            l_s[p, idx, :] = lsum
        return store

    wide = 2 * TQ
    ones_w = _head_ones(wide)
    bias_first = _band_bias(0, wide)
    bias_mid = _band_bias(RADIUS, wide)
    bias_last = _band_bias(2 * RADIUS, wide)

    attend(qn, kn, vn, 0, 0, wide, bias_first, ones_w, store_plane(0, pl.ds(0, TQ)))

    def p1_body(i, carry):
        q0 = pl.multiple_of(i * TQ, TQ)
        w0 = pl.multiple_of(i * TQ - RADIUS, RADIUS)
        attend(qn, kn, vn, q0, w0, wide, bias_mid, ones_w, store_plane(0, pl.ds(q0, TQ)))
        return carry

    lax.fori_loop(1, SEQ // TQ - 1, p1_body, 0)
    attend(qn, kn, vn, SEQ - TQ, SEQ - wide, wide, bias_last, ones_w,
           store_plane(0, pl.ds(SEQ - TQ, TQ)))

    qf[...] = qn[...].astype(F32)
    kf[...] = kn[...].astype(F32)
    vf[...] = vn[...].astype(F32)

    def gather_classes(d):
        L = SEQ // d
        for r in range(d):
            dst = pl.ds(r * L, L)
            src = pl.ds(r, L, stride=d)
            qc[dst, :] = qf[src, :].astype(BF16)
            kc[dst, :] = kf[src, :].astype(BF16)
            vc[dst, :] = vf[src, :].astype(BF16)

    gather_classes(4)
    L4 = SEQ // 4
    starts = (0, RADIUS, L4 - wide - RADIUS, L4 - wide)
    biases = (bias_first, bias_mid, bias_mid, bias_last)

    def p2_body(r, carry):
        base = pl.multiple_of(r * L4, L4)
        for j in range(L4 // TQ):
            attend(qc, kc, vc, base + j * TQ, base + starts[j], wide, biases[j], ones_w,
                   store_plane(1, pl.ds(r + 4 * j * TQ, TQ, stride=4)))
        return carry

    lax.fori_loop(0, 4, p2_body, 0)

    gather_classes(16)
    ones_n = _head_ones(TQ)
    bias_n = _band_bias(0, TQ)

    def p3_body(r, carry):
        base = pl.multiple_of(r * TQ, TQ)
        attend(qc, kc, vc, base, base, TQ, bias_n, ones_n,
               store_plane(2, pl.ds(r, TQ, stride=16)))
        return carry

    lax.fori_loop(0, 16, p3_body, 0)

    def merge_body(i, carry):
        rows = pl.ds(pl.multiple_of(i * TQ, TQ), TQ)
        m0, m1, m2 = m_s[0, rows, :], m_s[1, rows, :], m_s[2, rows, :]
        mx = jnp.maximum(jnp.maximum(m0, m1), m2)
        e0, e1, e2 = jnp.exp(m0 - mx), jnp.exp(m1 - mx), jnp.exp(m2 - mx)
        num = e0 * acc_s[0, rows, :] + e1 * acc_s[1, rows, :] + e2 * acc_s[2, rows, :]
        den = e0 * l_s[0, rows, :] + e1 * l_s[1, rows, :] + e2 * l_s[2, rows, :]
        o_ref[0, rows, :] = (num / den * g_ref[0, rows, :].astype(F32)).astype(BF16)
        return carry

    lax.fori_loop(0, SEQ // TQ, merge_body, 0)


def _attention(proj):
    B = proj.shape[0]
    nb = A_WIDTH // LANES

    def col(base):
        return pl.BlockSpec((1, SEQ, LANES), lambda b, g: (b, 0, base // LANES + g))

    return pl.pallas_call(
        _attn_kernel,
        grid=(B, nb),
        in_specs=[col(COL_QA), col(COL_KA), col(COL_VA), col(COL_GA)],
        out_specs=pl.BlockSpec((1, SEQ, LANES), lambda b, g: (b, 0, g)),
        out_shape=jax.ShapeDtypeStruct((B, SEQ, A_WIDTH), BF16),
        scratch_shapes=[pltpu.VMEM((SEQ, LANES), F32)] * 3
        + [pltpu.VMEM((SEQ, LANES), BF16)] * 3
        + [pltpu.VMEM((3, SEQ, LANES), F32)] * 3,
        compiler_params=pltpu.CompilerParams(
            dimension_semantics=("parallel", "parallel"), vmem_limit_bytes=VMEM_LIMIT),
        name="dilated_attn",
    )(proj, proj, proj, proj)


def _ret_kernel(lgf_ref, lgb_ref, q_ref, k_ref, v_ref, g_ref, o_ref, st_ref):
    gp = pl.program_id(1)
    lf0, lf1 = lgf_ref[2 * gp], lgf_ref[2 * gp + 1]
    lb0, lb1 = lgb_ref[2 * gp], lgb_ref[2 * gp + 1]

    row = lax.broadcasted_iota(jnp.int32, (CHUNK, LANES), 0)
    lane = lax.broadcasted_iota(jnp.int32, (CHUNK, LANES), 1)
    t = row.astype(F32)
    lf = jnp.where(lane < HEAD_DIM, lf0, lf1)
    lb = jnp.where(lane < HEAD_DIM, lb0, lb1)
    kdec_f = jnp.exp((CHUNK - 1.0 - t) * lf)
    kdec_b = jnp.exp(t * lb)
    qdec_f = jnp.exp((t + 1.0) * lf)
    qdec_b = jnp.exp((CHUNK - t) * lb)
    diff = (row - lane).astype(F32)
    dmat = [jnp.exp(jnp.where(diff >= 0, diff * a, -diff * b))
            for a, b in ((lf0, lb0), (lf1, lb1))]

    srow = lax.broadcasted_iota(jnp.int32, (LANES, 2 * LANES), 0)
    scol = lax.broadcasted_iota(jnp.int32, (LANES, 2 * LANES), 1)
    same_head = (srow < HEAD_DIM) == (scol < LANES)
    gch_f = jnp.exp(CHUNK * jnp.where(srow < HEAD_DIM, lf0, lf1))
    gch_b = jnp.exp(CHUNK * jnp.where(srow < HEAD_DIM, lb0, lb1))

    def chunk_rows(n):
        return pl.ds(pl.multiple_of(n * CHUNK, CHUNK), CHUNK)

    def chunk_kv(n, kdec):
        rows = chunk_rows(n)
        kd = (k_ref[0, rows, :].astype(F32) * kdec).astype(BF16)
        kv = lax.dot_general(kd, v_ref[0, rows, :], (((0,), (0,)), ((), ())),
                             preferred_element_type=F32)
        return jnp.where(same_head, kv, 0.0)

    def fwd_body(n, state):
        st_ref[n, 0:LANES, :] = state.astype(BF16)
        return gch_f * state + chunk_kv(n, kdec_f)

    lax.fori_loop(0, N_CHUNKS, fwd_body, jnp.zeros((LANES, 2 * LANES), F32))

    def bwd_body(i, state):
        n = N_CHUNKS - 1 - i
        st_ref[n, LANES:2 * LANES, :] = state.astype(BF16)
        return gch_b * state + chunk_kv(n, kdec_b)

    lax.fori_loop(0, N_CHUNKS, bwd_body, jnp.zeros((LANES, 2 * LANES), F32))

    def out_body(n, carry):
        rows = chunk_rows(n)
        q = q_ref[0, rows, :]
        k = k_ref[0, rows, :]
        v = v_ref[0, rows, :]
        qf32 = q.astype(F32)
        qd = jnp.concatenate([(qf32 * qdec_f).astype(BF16), (qf32 * qdec_b).astype(BF16)], axis=1)
        cross = jnp.dot(qd, st_ref[n], preferred_element_type=F32)
        head0 = lax.broadcasted_iota(jnp.int32, q.shape, 1) < HEAD_DIM
        zq = jnp.zeros_like(q)
        for h in range(2):
            qh = jnp.where(head0, q, zq) if h == 0 else jnp.where(head0, zq, q)
            s = lax.dot_general(qh, k, (((1,), (1,)), ((), ())), preferred_element_type=F32)
            inner = (s * dmat[h]).astype(BF16)
            cols = slice(h * LANES, (h + 1) * LANES)
            r = cross[:, cols] + jnp.dot(inner, v[:, cols], preferred_element_type=F32)
            r = r * lax.rsqrt(jnp.mean(r * r, axis=-1, keepdims=True) + EPS)
            o_ref[0, rows, cols] = (r * g_ref[0, rows, cols].astype(F32)).astype(BF16)
        return carry

    lax.fori_loop(0, N_CHUNKS, out_body, 0)


def _retention(proj, lg_f, lg_b):
    B = proj.shape[0]
    smem = pl.BlockSpec(memory_space=pltpu.SMEM)
    return pl.pallas_call(
        _ret_kernel,
        grid=(B, R_HEADS // 2),
        in_specs=[
            smem, smem,
            pl.BlockSpec((1, SEQ, LANES), lambda b, g: (b, 0, COL_QB // LANES + g)),
            pl.BlockSpec((1, SEQ, LANES), lambda b, g: (b, 0, COL_KB // LANES + g)),
            pl.BlockSpec((1, SEQ, 2 * LANES), lambda b, g: (b, 0, COL_VB // (2 * LANES) + g)),
            pl.BlockSpec((1, SEQ, 2 * LANES), lambda b, g: (b, 0, COL_GB // (2 * LANES) + g)),
        ],
        out_specs=pl.BlockSpec((1, SEQ, 2 * LANES), lambda b, g: (b, 0, g)),
        out_shape=jax.ShapeDtypeStruct((B, SEQ, R_WIDTH), BF16),
        scratch_shapes=[pltpu.VMEM((N_CHUNKS, 2 * LANES, 2 * LANES), BF16)],
        compiler_params=pltpu.CompilerParams(
            dimension_semantics=("parallel", "parallel"), vmem_limit_bytes=VMEM_LIMIT),
        name="retention",
    )(lg_f, lg_b, proj, proj, proj, proj)


def _outproj_kernel(ya_ref, yr_ref, x_ref, gate_ref, w_ref, gfin_ref, o_ref, *, final):
    mix = jnp.concatenate([ya_ref[0], yr_ref[0]], axis=1)
    out = jnp.dot(mix, w_ref[...], preferred_element_type=F32)
    xn = x_ref[0] + gate_ref[0] * out
    if final:
        ms = jnp.mean(xn * xn, axis=-1, keepdims=True)
        xn = xn * lax.rsqrt(ms + EPS) * gfin_ref[...]
    o_ref[0] = xn


def _out_projection(ya, yr, x, mod3, w_bf16, g_final, final):
    B = x.shape[0]
    tm = ROW_TILE
    return pl.pallas_call(
        functools.partial(_outproj_kernel, final=final),
        grid=(B, SEQ // tm),
        in_specs=[
            pl.BlockSpec((1, tm, A_WIDTH), lambda b, i: (b, i, 0)),
            pl.BlockSpec((1, tm, R_WIDTH), lambda b, i: (b, i, 0)),
            pl.BlockSpec((1, tm, D_MODEL), lambda b, i: (b, i, 0)),
            pl.BlockSpec((1, 1, D_MODEL), lambda b, i: (b, 0, 2)),
            pl.BlockSpec((MIX_WIDTH, D_MODEL), lambda b, i: (0, 0)),
            pl.BlockSpec((1, D_MODEL), lambda b, i: (0, 0)),
        ],
        out_specs=pl.BlockSpec((1, tm, D_MODEL), lambda b, i: (b, i, 0)),
        out_shape=jax.ShapeDtypeStruct((B, SEQ, D_MODEL), F32),
        compiler_params=pltpu.CompilerParams(
            dimension_semantics=("parallel", "parallel"), vmem_limit_bytes=VMEM_LIMIT),
        name="out_proj",
    )(ya, yr, x, mod3, w_bf16, g_final)


def _rope_tables():
    inv = ROPE_THETA ** (-jnp.arange(0, HEAD_DIM, 2, dtype=F32) / HEAD_DIM)
    ang = jnp.arange(SEQ, dtype=F32)[:, None] * inv[None, :]
    cos, sin = jnp.cos(ang), jnp.sin(ang)
    return jnp.tile(cos, (1, 4)), jnp.concatenate([-sin, sin, -sin, sin], axis=1)


def _trunk(x, c, g_norm, w_ada, b_ada, w_in_bf16, w_out_bf16, lg_f, lg_b, g_final, tables):
    B = x.shape[0]
    mod = _modulation(c, w_ada, b_ada)
    cos_t, sin_t = tables
    for l in range(DEPTH):
        mod3 = mod[l].reshape(B, 1, 3 * D_MODEL)
        proj = _in_projection(x, mod3, g_norm[l].reshape(1, D_MODEL), w_in_bf16[l], cos_t, sin_t)
        ya = _attention(proj)
        yr = _retention(proj, lg_f[l], lg_b[l])
        x = _out_projection(ya, yr, x, mod3, w_out_bf16[l], g_final.reshape(1, D_MODEL),
                            final=(l == DEPTH - 1))
    return x


def kernel(x_prompt, x_sample, c_prompt, c_sample, g_norm, w_ada, b_ada, w_in, w_out,
           decay_fwd, decay_bwd, g_final):
    w_in_bf16 = w_in.astype(BF16)
    w_out_bf16 = w_out.astype(BF16)
    lg_f = jax.nn.log_sigmoid(decay_fwd.astype(F32))
    lg_b = jax.nn.log_sigmoid(decay_bwd.astype(F32))
    tables = _rope_tables()
    args = (g_norm, w_ada, b_ada, w_in_bf16, w_out_bf16, lg_f, lg_b, g_final, tables)
    return (_trunk(x_prompt, c_prompt, *args), _trunk(x_sample, c_sample, *args))
```

```python
import functools
import math

import jax
import jax.numpy as jnp
from jax import lax
from jax.experimental import pallas as pl
from jax.experimental.pallas import tpu as pltpu

F32 = jnp.float32
BF16 = jnp.bfloat16

D_MODEL = 1024
SEQ = 2048
DEPTH = 2
HEAD_DIM = 64
A_WIDTH = 512
R_HEADS = 4
R_WIDTH = 512
MIX_WIDTH = 1024
IN_WIDTH = 3584
CHUNK = 128
N_CHUNKS = SEQ // CHUNK
ROPE_THETA = 10000.0
EPS = 1e-6
NEG = -1e30
RADIUS = 64
LOG2E = math.log2(math.e)

LANES = 128
TQ = 128
ATTN_GROUP = 4
RET_GROUP = 4
ROW_TILE = 512
COL_CHUNK = 512
VMEM_LIMIT = 48 * 1024 * 1024

COL_QA, COL_KA, COL_VA, COL_GA = 0, 512, 1024, 1536
COL_QB, COL_KB, COL_VB, COL_GB = 2048, 2304, 2560, 3072


def _sigmoid(x):
    return 1.0 / (1.0 + jnp.exp(-x))


def _mod_kernel(c_ref, w_ref, b_ref, o_ref):
    c = c_ref[...]
    a = c * _sigmoid(c)
    o_ref[0] = jnp.dot(a, w_ref[0], preferred_element_type=F32,
                       precision=lax.Precision.HIGHEST) + b_ref[0]


def _modulation(c, w_ada, b_ada):
    B = c.shape[0]
    nt = 512
    return pl.pallas_call(
        _mod_kernel,
        grid=(DEPTH, 3 * D_MODEL // nt),
        in_specs=[
            pl.BlockSpec((B, D_MODEL), lambda l, j: (0, 0)),
            pl.BlockSpec((1, D_MODEL, nt), lambda l, j: (l, 0, j)),
            pl.BlockSpec((1, 1, nt), lambda l, j: (l, 0, j)),
        ],
        out_specs=pl.BlockSpec((1, B, nt), lambda l, j: (l, 0, j)),
        out_shape=jax.ShapeDtypeStruct((DEPTH, B, 3 * D_MODEL), F32),
        name="adaln_mod",
    )(c, w_ada, b_ada.reshape(DEPTH, 1, 3 * D_MODEL))


def _inproj_kernel(x_ref, shift_ref, scale_ref, g_ref, w_ref, cos_ref, sin_ref, o_ref):
    x = x_ref[0]
    ms = jnp.mean(x * x, axis=-1, keepdims=True)
    h = x * lax.rsqrt(ms + EPS) * g_ref[...]
    h = h * (1.0 + scale_ref[0]) + shift_ref[0]
    hb = h.astype(BF16)
    cos = cos_ref[...]
    sin = sin_ref[...]
    lane = lax.broadcasted_iota(jnp.int32, cos.shape, 1)
    first_half = (lane & 32) == 0
    for c in range(IN_WIDTH // COL_CHUNK):
        acc = jnp.dot(hb, w_ref[:, c * COL_CHUNK:(c + 1) * COL_CHUNK],
                      preferred_element_type=F32)
        for j in range(COL_CHUNK // LANES):
            col = c * COL_CHUNK + j * LANES
            a = acc[:, j * LANES:(j + 1) * LANES]
            if col < COL_VA or COL_QB <= col < COL_VB:
                partner = jnp.where(first_half, pltpu.roll(a, 96, 1), pltpu.roll(a, 32, 1))
                a = a * cos + partner * sin
            if col < COL_KA:
                a = a * (HEAD_DIM ** -0.5 * LOG2E)
            if COL_KB <= col < COL_VB:
                a = a * (HEAD_DIM ** -0.5)
            if COL_GA <= col < COL_QB or col >= COL_GB:
                a = a * _sigmoid(a)
            o_ref[0, :, col:col + LANES] = a.astype(BF16)


def _in_projection(x, mod3, g, w_bf16, cos_t, sin_t):
    B = x.shape[0]
    tm = ROW_TILE
    return pl.pallas_call(
        _inproj_kernel,
        grid=(B, SEQ // tm),
        in_specs=[
            pl.BlockSpec((1, tm, D_MODEL), lambda b, i: (b, i, 0)),
            pl.BlockSpec((1, 1, D_MODEL), lambda b, i: (b, 0, 0)),
            pl.BlockSpec((1, 1, D_MODEL), lambda b, i: (b, 0, 1)),
            pl.BlockSpec((1, D_MODEL), lambda b, i: (0, 0)),
            pl.BlockSpec((D_MODEL, IN_WIDTH), lambda b, i: (0, 0)),
            pl.BlockSpec((tm, LANES), lambda b, i: (i, 0)),
            pl.BlockSpec((tm, LANES), lambda b, i: (i, 0)),
        ],
        out_specs=pl.BlockSpec((1, tm, IN_WIDTH), lambda b, i: (b, i, 0)),
        out_shape=jax.ShapeDtypeStruct((B, SEQ, IN_WIDTH), BF16),
        compiler_params=pltpu.CompilerParams(
            dimension_semantics=("parallel", "parallel"), vmem_limit_bytes=VMEM_LIMIT),
        name="in_proj",
    )(x, mod3, mod3, g, w_bf16, cos_t, sin_t)


def _band_bias(off, width):
    r = lax.broadcasted_iota(jnp.int32, (TQ, width), 0)
    c = lax.broadcasted_iota(jnp.int32, (TQ, width), 1)
    d = r + off - c
    return jnp.where((d <= RADIUS) & (d >= -RADIUS), 0.0, NEG).astype(F32)


def _head_ones(width):
    r = lax.broadcasted_iota(jnp.int32, (2 * width, LANES), 0)
    c = lax.broadcasted_iota(jnp.int32, (2 * width, LANES), 1)
    return jnp.where((r < width) == (c < HEAD_DIM), 1.0, 0.0).astype(BF16)


def _attn_kernel(q_ref, k_ref, v_ref, g_ref, o_ref,
                 qf, kf, vf, qf4, kf4, vf4, qc, kc, vc, bias_s, ones_s, acc_s, m_s, l_s):
    qn, kn, vn = q_ref.at[0], k_ref.at[0], v_ref.at[0]
    wide = 2 * TQ

    def attend_group(qr, kr, vr, blocks, width, ones):
        scores = []
        for q0, w0, bias, _ in blocks:
            q = qr[pl.ds(q0, TQ), :]
            head0 = lax.broadcasted_iota(jnp.int32, q.shape, 1) < HEAD_DIM
            zq = jnp.zeros_like(q)
            qs = jnp.concatenate([jnp.where(head0, q, zq), jnp.where(head0, zq, q)], axis=0)
            kw = kr[pl.ds(w0, width), :]
            s = lax.dot_general(qs, kw, (((1,), (1,)), ((), ())),
                                preferred_element_type=F32)
            scores.append(s + jnp.concatenate([bias, bias], axis=0))
        maxes = [jnp.max(s, axis=-1, keepdims=True) for s in scores]
        probs = [jnp.exp2(s - m).astype(BF16) for s, m in zip(scores, maxes)]
        for (q0, w0, _, store), p, m in zip(blocks, probs, maxes):
            pc = jnp.concatenate([p[:TQ], p[TQ:]], axis=1)
            vw = vr[pl.ds(w0, width), :]
            vhead0 = lax.broadcasted_iota(jnp.int32, vw.shape, 1) < HEAD_DIM
            zv = jnp.zeros_like(vw)
            vs = jnp.concatenate([jnp.where(vhead0, vw, zv), jnp.where(vhead0, zv, vw)], axis=0)
            r = jnp.dot(pc, jnp.concatenate([vs, ones[...]], axis=1),
                        preferred_element_type=F32)
            h0f = lax.broadcasted_iota(jnp.int32, (TQ, LANES), 1) < HEAD_DIM
            store(r[:, :LANES], jnp.where(h0f, m[:TQ], m[TQ:]), r[:, LANES:])

    def store_plane(p, idx):
        def store(acc, mexp, lsum):
            acc_s[p, idx, :] = acc
            m_s[p, idx, :] = mexp
            l_s[p, idx, :] = lsum
        return store

    def banded_block(q_local, base, length):
        w_local = jnp.clip(q_local - RADIUS, 0, length - wide)
        kind = (q_local - w_local) // RADIUS
        return base + q_local, pl.multiple_of(base + w_local, RADIUS), bias_s[kind]

    for kind in range(3):
        bias_s[kind] = _band_bias(kind * RADIUS, wide)
    ones_s[...] = _head_ones(wide)
    ones_w = ones_s

    def p1_body(it, carry):
        blocks = []
        for g in range(ATTN_GROUP):
            q_local = pl.multiple_of((it * ATTN_GROUP + g) * TQ, TQ)
            q0, w0, bias = banded_block(q_local, 0, SEQ)
            blocks.append((q0, w0, bias, store_plane(0, pl.ds(q0, TQ))))
        attend_group(qn, kn, vn, blocks, wide, ones_w)
        return carry

    lax.fori_loop(0, SEQ // TQ // ATTN_GROUP, p1_body, 0)

    qf[...] = qn[...].astype(F32)
    kf[...] = kn[...].astype(F32)
    vf[...] = vn[...].astype(F32)
    L4 = SEQ // 4
    for src32, dst32, dst16 in ((qf, qf4, qc), (kf, kf4, kc), (vf, vf4, vc)):
        for r in range(4):
            x = src32[pl.ds(r, L4, stride=4), :]
            dst32[pl.ds(r * L4, L4), :] = x
            dst16[pl.ds(r * L4, L4), :] = x.astype(BF16)

    def p2_body(it, carry):
        blocks = []
        for g in range(ATTN_GROUP):
            blk = it * ATTN_GROUP + g
            r = blk // (L4 // TQ)
            j = blk % (L4 // TQ)
            q0, w0, bias = banded_block(pl.multiple_of(j * TQ, TQ), pl.multiple_of(r * L4, L4), L4)
            blocks.append((q0, w0, bias, store_plane(1, pl.ds(r + 4 * j * TQ, TQ, stride=4))))
        attend_group(qc, kc, vc, blocks, wide, ones_w)
        return carry

    lax.fori_loop(0, SEQ // TQ // ATTN_GROUP, p2_body, 0)

    L16 = SEQ // 16
    for src32, dst16 in ((qf4, qc), (kf4, kc), (vf4, vc)):
        for r4 in range(4):
            for a in range(4):
                dst16[pl.ds((4 * a + r4) * L16, L16), :] = (
                    src32[pl.ds(r4 * L4 + a, L16, stride=4), :].astype(BF16))
    ones_n = ones_s.at[pl.ds(TQ, wide)]
    bias_n = _band_bias(0, TQ)

    def p3_body(it, carry):
        blocks = []
        for g in range(ATTN_GROUP):
            r = it * ATTN_GROUP + g
            base = pl.multiple_of(r * TQ, TQ)
            blocks.append((base, base, bias_n, store_plane(2, pl.ds(r, TQ, stride=16))))
        attend_group(qc, kc, vc, blocks, TQ, ones_n)
        return carry

    lax.fori_loop(0, 16 // ATTN_GROUP, p3_body, 0)

    def merge_body(i, carry):
        rows = pl.ds(pl.multiple_of(i * TQ, TQ), TQ)
        m0, m1, m2 = m_s[0, rows, :], m_s[1, rows, :], m_s[2, rows, :]
        mx = jnp.maximum(jnp.maximum(m0, m1), m2)
        e0, e1, e2 = jnp.exp2(m0 - mx), jnp.exp2(m1 - mx), jnp.exp2(m2 - mx)
        num = e0 * acc_s[0, rows, :] + e1 * acc_s[1, rows, :] + e2 * acc_s[2, rows, :]
        den = e0 * l_s[0, rows, :] + e1 * l_s[1, rows, :] + e2 * l_s[2, rows, :]
        o_ref[0, rows, :] = (num / den * g_ref[0, rows, :].astype(F32)).astype(BF16)
        return carry

    lax.fori_loop(0, SEQ // TQ, merge_body, 0)


def _attention(proj):
    B = proj.shape[0]
    nb = A_WIDTH // LANES

    def col(base):
        return pl.BlockSpec((1, SEQ, LANES), lambda b, g: (b, 0, base // LANES + g))

    return pl.pallas_call(
        _attn_kernel,
        grid=(B, nb),
        in_specs=[col(COL_QA), col(COL_KA), col(COL_VA), col(COL_GA)],
        out_specs=pl.BlockSpec((1, SEQ, LANES), lambda b, g: (b, 0, g)),
        out_shape=jax.ShapeDtypeStruct((B, SEQ, A_WIDTH), BF16),
        scratch_shapes=[pltpu.VMEM((SEQ, LANES), F32)] * 6
        + [pltpu.VMEM((SEQ, LANES), BF16)] * 3
        + [pltpu.VMEM((3, TQ, 2 * TQ), F32), pltpu.VMEM((4 * TQ, LANES), BF16)]
        + [pltpu.VMEM((3, SEQ, LANES), F32)] * 3,
        compiler_params=pltpu.CompilerParams(
            dimension_semantics=("parallel", "parallel"), vmem_limit_bytes=VMEM_LIMIT),
        name="dilated_attn",
    )(proj, proj, proj, proj)


KDEC_F, KDEC_B, QDEC_F, QDEC_B, DMAT0 = 0, 1, 2, 3, 4
GCH_F, GCH_B, SAME_HEAD = 0, 1, 2


def _ret_kernel(lgf_ref, lgb_ref, q_ref, k_ref, v_ref, g_ref, o_ref, tab_s, gch_s, kv_s, st_ref):
    gp = pl.program_id(1)
    lf0, lf1 = lgf_ref[2 * gp], lgf_ref[2 * gp + 1]
    lb0, lb1 = lgb_ref[2 * gp], lgb_ref[2 * gp + 1]

    row = lax.broadcasted_iota(jnp.int32, (CHUNK, LANES), 0)
    lane = lax.broadcasted_iota(jnp.int32, (CHUNK, LANES), 1)
    t = row.astype(F32)
    lf = jnp.where(lane < HEAD_DIM, lf0, lf1)
    lb = jnp.where(lane < HEAD_DIM, lb0, lb1)
    diff = (row - lane).astype(F32)
    tab_s[KDEC_F] = jnp.exp((CHUNK - 1.0 - t) * lf)
    tab_s[KDEC_B] = jnp.exp(t * lb)
    tab_s[QDEC_F] = jnp.exp((t + 1.0) * lf)
    tab_s[QDEC_B] = jnp.exp((CHUNK - t) * lb)
    tab_s[DMAT0] = jnp.exp(jnp.where(diff >= 0, diff * lf0, -diff * lb0))
    tab_s[DMAT0 + 1] = jnp.exp(jnp.where(diff >= 0, diff * lf1, -diff * lb1))

    srow = lax.broadcasted_iota(jnp.int32, (LANES, 2 * LANES), 0)
    scol = lax.broadcasted_iota(jnp.int32, (LANES, 2 * LANES), 1)
    gch_s[GCH_F] = jnp.exp(CHUNK * jnp.where(srow < HEAD_DIM, lf0, lf1))
    gch_s[GCH_B] = jnp.exp(CHUNK * jnp.where(srow < HEAD_DIM, lb0, lb1))
    gch_s[SAME_HEAD] = jnp.where((srow < HEAD_DIM) == (scol < LANES), 1.0, 0.0)

    def chunk_rows(n):
        return pl.ds(pl.multiple_of(n * CHUNK, CHUNK), CHUNK)

    def kv_body(n, carry):
        rows = chunk_rows(n)
        kf32 = k_ref[0, rows, :].astype(F32)
        kd = jnp.concatenate([(kf32 * tab_s[KDEC_F]).astype(BF16),
                              (kf32 * tab_s[KDEC_B]).astype(BF16)], axis=1)
        kv = lax.dot_general(kd, v_ref[0, rows, :], (((0,), (0,)), ((), ())),
                             preferred_element_type=F32)
        kv_s[n, 0:LANES, :] = kv[:LANES] * gch_s[SAME_HEAD]
        kv_s[n, LANES:, :] = kv[LANES:] * gch_s[SAME_HEAD]
        return carry

    lax.fori_loop(0, N_CHUNKS, kv_body, 0, unroll=RET_GROUP)

    def scan_body(n, carry):
        sf, sb = carry
        nb = N_CHUNKS - 1 - n
        st_ref[n, 0:LANES, :] = sf.astype(BF16)
        st_ref[nb, LANES:, :] = sb.astype(BF16)
        return (gch_s[GCH_F] * sf + kv_s[n, 0:LANES, :], gch_s[GCH_B] * sb + kv_s[nb, LANES:, :])

    zero = jnp.zeros((LANES, 2 * LANES), F32)
    lax.fori_loop(0, N_CHUNKS, scan_body, (zero, zero))

    def out_body(n, carry):
        rows = chunk_rows(n)
        q = q_ref[0, rows, :]
        k = k_ref[0, rows, :]
        v = v_ref[0, rows, :]
        qf32 = q.astype(F32)
        qd = jnp.concatenate([(qf32 * tab_s[QDEC_F]).astype(BF16),
                              (qf32 * tab_s[QDEC_B]).astype(BF16)], axis=1)
        cross = jnp.dot(qd, st_ref[n], preferred_element_type=F32)
        head0 = lax.broadcasted_iota(jnp.int32, q.shape, 1) < HEAD_DIM
        zq = jnp.zeros_like(q)
        for h in range(2):
            qh = jnp.where(head0, q, zq) if h == 0 else jnp.where(head0, zq, q)
            s = lax.dot_general(qh, k, (((1,), (1,)), ((), ())), preferred_element_type=F32)
            inner = (s * tab_s[DMAT0 + h]).astype(BF16)
            cols = slice(h * LANES, (h + 1) * LANES)
            r = cross[:, cols] + jnp.dot(inner, v[:, cols], preferred_element_type=F32)
            r = r * lax.rsqrt(jnp.mean(r * r, axis=-1, keepdims=True) + EPS)
            o_ref[0, rows, cols] = (r * g_ref[0, rows, cols].astype(F32)).astype(BF16)
        return carry

    lax.fori_loop(0, N_CHUNKS, out_body, 0, unroll=RET_GROUP)


def _retention(proj, lg_f, lg_b):
    B = proj.shape[0]
    smem = pl.BlockSpec(memory_space=pltpu.SMEM)
    return pl.pallas_call(
        _ret_kernel,
        grid=(B, R_HEADS // 2),
        in_specs=[
            smem, smem,
            pl.BlockSpec((1, SEQ, LANES), lambda b, g: (b, 0, COL_QB // LANES + g)),
            pl.BlockSpec((1, SEQ, LANES), lambda b, g: (b, 0, COL_KB // LANES + g)),
            pl.BlockSpec((1, SEQ, 2 * LANES), lambda b, g: (b, 0, COL_VB // (2 * LANES) + g)),
            pl.BlockSpec((1, SEQ, 2 * LANES), lambda b, g: (b, 0, COL_GB // (2 * LANES) + g)),
        ],
        out_specs=pl.BlockSpec((1, SEQ, 2 * LANES), lambda b, g: (b, 0, g)),
        out_shape=jax.ShapeDtypeStruct((B, SEQ, R_WIDTH), BF16),
        scratch_shapes=[pltpu.VMEM((6, CHUNK, LANES), F32),
                        pltpu.VMEM((3, LANES, 2 * LANES), F32),
                        pltpu.VMEM((N_CHUNKS, 2 * LANES, 2 * LANES), F32),
                        pltpu.VMEM((N_CHUNKS, 2 * LANES, 2 * LANES), BF16)],
        compiler_params=pltpu.CompilerParams(
            dimension_semantics=("parallel", "parallel"), vmem_limit_bytes=VMEM_LIMIT),
        name="retention",
    )(lg_f, lg_b, proj, proj, proj, proj)


def _outproj_kernel(ya_ref, yr_ref, x_ref, gate_ref, w_ref, gfin_ref, o_ref, *, final):
    mix = jnp.concatenate([ya_ref[0], yr_ref[0]], axis=1)
    out = jnp.dot(mix, w_ref[...], preferred_element_type=F32)
    xn = x_ref[0] + gate_ref[0] * out
    if final:
        ms = jnp.mean(xn * xn, axis=-1, keepdims=True)
        xn = xn * lax.rsqrt(ms + EPS) * gfin_ref[...]
    o_ref[0] = xn


def _out_projection(ya, yr, x, mod3, w_bf16, g_final, final):
    B = x.shape[0]
    tm = ROW_TILE
    return pl.pallas_call(
        functools.partial(_outproj_kernel, final=final),
        grid=(B, SEQ // tm),
        in_specs=[
            pl.BlockSpec((1, tm, A_WIDTH), lambda b, i: (b, i, 0)),
            pl.BlockSpec((1, tm, R_WIDTH), lambda b, i: (b, i, 0)),
            pl.BlockSpec((1, tm, D_MODEL), lambda b, i: (b, i, 0)),
            pl.BlockSpec((1, 1, D_MODEL), lambda b, i: (b, 0, 2)),
            pl.BlockSpec((MIX_WIDTH, D_MODEL), lambda b, i: (0, 0)),
            pl.BlockSpec((1, D_MODEL), lambda b, i: (0, 0)),
        ],
        out_specs=pl.BlockSpec((1, tm, D_MODEL), lambda b, i: (b, i, 0)),
        out_shape=jax.ShapeDtypeStruct((B, SEQ, D_MODEL), F32),
        compiler_params=pltpu.CompilerParams(
            dimension_semantics=("parallel", "parallel"), vmem_limit_bytes=VMEM_LIMIT),
        name="out_proj",
    )(ya, yr, x, mod3, w_bf16, g_final)


def _rope_tables():
    inv = ROPE_THETA ** (-jnp.arange(0, HEAD_DIM, 2, dtype=F32) / HEAD_DIM)
    ang = jnp.arange(SEQ, dtype=F32)[:, None] * inv[None, :]
    cos, sin = jnp.cos(ang), jnp.sin(ang)
    return jnp.tile(cos, (1, 4)), jnp.concatenate([-sin, sin, -sin, sin], axis=1)


def _trunk(x, c, g_norm, w_ada, b_ada, w_in_bf16, w_out_bf16, lg_f, lg_b, g_final, tables):
    B = x.shape[0]
    mod = _modulation(c, w_ada, b_ada)
    cos_t, sin_t = tables
    for l in range(DEPTH):
        mod3 = mod[l].reshape(B, 1, 3 * D_MODEL)
        proj = _in_projection(x, mod3, g_norm[l].reshape(1, D_MODEL), w_in_bf16[l], cos_t, sin_t)
        ya = _attention(proj)
        yr = _retention(proj, lg_f[l], lg_b[l])
        x = _out_projection(ya, yr, x, mod3, w_out_bf16[l], g_final.reshape(1, D_MODEL),
                            final=(l == DEPTH - 1))
    return x


def kernel(x_prompt, x_sample, c_prompt, c_sample, g_norm, w_ada, b_ada, w_in, w_out,
           decay_fwd, decay_bwd, g_final):
    w_in_bf16 = w_in.astype(BF16)
    w_out_bf16 = w_out.astype(BF16)
    lg_f = jax.nn.log_sigmoid(decay_fwd.astype(F32))
    lg_b = jax.nn.log_sigmoid(decay_bwd.astype(F32))
    tables = _rope_tables()
    args = (g_norm, w_ada, b_ada, w_in_bf16, w_out_bf16, lg_f, lg_b, g_final, tables)
    return (_trunk(x_prompt, c_prompt, *args), _trunk(x_sample, c_sample, *args))
```

```python
import functools
import math

import jax
import jax.numpy as jnp
from jax import lax
from jax.experimental import pallas as pl
from jax.experimental.pallas import tpu as pltpu

F32 = jnp.float32
BF16 = jnp.bfloat16

D_MODEL = 1024
SEQ = 2048
DEPTH = 2
HEAD_DIM = 64
A_WIDTH = 512
R_HEADS = 4
R_WIDTH = 512
MIX_WIDTH = 1024
IN_WIDTH = 3584
CHUNK = 128
N_CHUNKS = SEQ // CHUNK
ROPE_THETA = 10000.0
EPS = 1e-6
NEG = -1e30
RADIUS = 64
LOG2E = math.log2(math.e)

LANES = 128
TQ = 128
BLOCKS_PER_PATTERN = 1
ROW_TILE = 512
COL_CHUNK = 512
VMEM_LIMIT = 48 * 1024 * 1024

COL_QA, COL_KA, COL_VA, COL_GA = 0, 512, 1024, 1536
COL_QB, COL_KB, COL_VB, COL_GB = 2048, 2304, 2560, 3072


def _sigmoid(x):
    return 1.0 / (1.0 + jnp.exp(-x))


def _mod_kernel(c_ref, w_ref, b_ref, o_ref):
    c = c_ref[...]
    a = c * _sigmoid(c)
    o_ref[0] = jnp.dot(a, w_ref[0], preferred_element_type=F32,
                       precision=lax.Precision.HIGHEST) + b_ref[0]


def _modulation(c, w_ada, b_ada):
    B = c.shape[0]
    nt = 512
    return pl.pallas_call(
        _mod_kernel,
        grid=(DEPTH, 3 * D_MODEL // nt),
        in_specs=[
            pl.BlockSpec((B, D_MODEL), lambda l, j: (0, 0)),
            pl.BlockSpec((1, D_MODEL, nt), lambda l, j: (l, 0, j)),
            pl.BlockSpec((1, 1, nt), lambda l, j: (l, 0, j)),
        ],
        out_specs=pl.BlockSpec((1, B, nt), lambda l, j: (l, 0, j)),
        out_shape=jax.ShapeDtypeStruct((DEPTH, B, 3 * D_MODEL), F32),
        name="adaln_mod",
    )(c, w_ada, b_ada.reshape(DEPTH, 1, 3 * D_MODEL))


def _inproj_kernel(x_ref, shift_ref, scale_ref, g_ref, w_ref, cos_ref, sin_ref, o_ref):
    x = x_ref[0]
    ms = jnp.mean(x * x, axis=-1, keepdims=True)
    h = x * lax.rsqrt(ms + EPS) * g_ref[...]
    h = h * (1.0 + scale_ref[0]) + shift_ref[0]
    hb = h.astype(BF16)
    cos = cos_ref[...]
    sin = sin_ref[...]
    lane = lax.broadcasted_iota(jnp.int32, cos.shape, 1)
    first_half = (lane & 32) == 0
    for c in range(IN_WIDTH // COL_CHUNK):
        acc = jnp.dot(hb, w_ref[:, c * COL_CHUNK:(c + 1) * COL_CHUNK],
                      preferred_element_type=F32)
        for j in range(COL_CHUNK // LANES):
            col = c * COL_CHUNK + j * LANES
            a = acc[:, j * LANES:(j + 1) * LANES]
            if col < COL_VA or COL_QB <= col < COL_VB:
                partner = jnp.where(first_half, pltpu.roll(a, 96, 1), pltpu.roll(a, 32, 1))
                a = a * cos + partner * sin
            if col < COL_KA:
                a = a * (HEAD_DIM ** -0.5 * LOG2E)
            if COL_KB <= col < COL_VB:
                a = a * (HEAD_DIM ** -0.5)
            if COL_GA <= col < COL_QB or col >= COL_GB:
                a = a * _sigmoid(a)
            o_ref[0, :, col:col + LANES] = a.astype(BF16)


def _in_projection(x, mod3, g, w_bf16, cos_t, sin_t):
    B = x.shape[0]
    tm = ROW_TILE
    return pl.pallas_call(
        _inproj_kernel,
        grid=(B, SEQ // tm),
        in_specs=[
            pl.BlockSpec((1, tm, D_MODEL), lambda b, i: (b, i, 0)),
            pl.BlockSpec((1, 1, D_MODEL), lambda b, i: (b, 0, 0)),
            pl.BlockSpec((1, 1, D_MODEL), lambda b, i: (b, 0, 1)),
            pl.BlockSpec((1, D_MODEL), lambda b, i: (0, 0)),
            pl.BlockSpec((D_MODEL, IN_WIDTH), lambda b, i: (0, 0)),
            pl.BlockSpec((tm, LANES), lambda b, i: (i, 0)),
            pl.BlockSpec((tm, LANES), lambda b, i: (i, 0)),
        ],
        out_specs=pl.BlockSpec((1, tm, IN_WIDTH), lambda b, i: (b, i, 0)),
        out_shape=jax.ShapeDtypeStruct((B, SEQ, IN_WIDTH), BF16),
        compiler_params=pltpu.CompilerParams(
            dimension_semantics=("parallel", "parallel"), vmem_limit_bytes=VMEM_LIMIT),
        name="in_proj",
    )(x, mod3, mod3, g, w_bf16, cos_t, sin_t)


def _band_bias(off, width):
    r = lax.broadcasted_iota(jnp.int32, (TQ, width), 0)
    c = lax.broadcasted_iota(jnp.int32, (TQ, width), 1)
    d = r + off - c
    return jnp.where((d <= RADIUS) & (d >= -RADIUS), 0.0, NEG).astype(F32)


def _head_ones(width):
    r = lax.broadcasted_iota(jnp.int32, (2 * width, LANES), 0)
    c = lax.broadcasted_iota(jnp.int32, (2 * width, LANES), 1)
    return jnp.where((r < width) == (c < HEAD_DIM), 1.0, 0.0).astype(BF16)


def _attn_kernel(q_ref, k_ref, v_ref, g_ref, o_ref,
                 qf, kf, vf, qf4, kf4, vf4, q4, k4, v4, q16, k16, v16,
                 bias_s, biasn_s, ones_s, acc_s, m_s, l_s):
    qn, kn, vn = q_ref.at[0], k_ref.at[0], v_ref.at[0]
    wide = 2 * TQ

    def attend(b):
        width = b["width"]
        q = b["q"][pl.ds(b["q0"], TQ), :]
        head0 = lax.broadcasted_iota(jnp.int32, q.shape, 1) < HEAD_DIM
        zq = jnp.zeros_like(q)
        qs = jnp.concatenate([jnp.where(head0, q, zq), jnp.where(head0, zq, q)], axis=0)
        kw = b["k"][pl.ds(b["w0"], width), :]
        s = lax.dot_general(qs, kw, (((1,), (1,)), ((), ())), preferred_element_type=F32)
        bias = b["bias"]()
        s = s + jnp.concatenate([bias, bias], axis=0)
        m = jnp.max(s, axis=-1, keepdims=True)
        p = jnp.exp2(s - m).astype(BF16)
        pc = jnp.concatenate([p[:TQ], p[TQ:]], axis=1)
        vw = b["v"][pl.ds(b["w0"], width), :]
        vhead0 = lax.broadcasted_iota(jnp.int32, vw.shape, 1) < HEAD_DIM
        zv = jnp.zeros_like(vw)
        vs = jnp.concatenate([jnp.where(vhead0, vw, zv), jnp.where(vhead0, zv, vw)], axis=0)
        r = jnp.dot(pc, jnp.concatenate([vs, b["ones"][...]], axis=1),
                    preferred_element_type=F32)
        h0f = lax.broadcasted_iota(jnp.int32, (TQ, LANES), 1) < HEAD_DIM
        m_s[b["plane"], b["rows"], :] = jnp.where(h0f, m[:TQ], m[TQ:])
        acc_s[b["plane"], b["rows"], :] = r[:, :LANES]
        l_s[b["plane"], b["rows"], :] = r[:, LANES:]

    def aligned(x, m):
        return x if isinstance(x, int) else pl.multiple_of(x, m)

    def banded_block(q_local, base, length):
        if isinstance(q_local, int):
            w_local = min(max(q_local - RADIUS, 0), length - wide)
        else:
            w_local = jnp.clip(q_local - RADIUS, 0, length - wide)
        kind = (q_local - w_local) // RADIUS
        return base + q_local, aligned(base + w_local, RADIUS), lambda: bias_s[kind]

    for kind in range(3):
        bias_s[kind] = _band_bias(kind * RADIUS, wide)
    biasn_s[...] = _band_bias(0, TQ)
    ones_s[...] = _head_ones(wide)
    ones_n = ones_s.at[pl.ds(TQ, wide)]

    qf[...] = qn[...].astype(F32)
    kf[...] = kn[...].astype(F32)
    vf[...] = vn[...].astype(F32)
    L4, L16 = SEQ // 4, SEQ // 16
    for src32, mid32, dst4, dst16 in ((qf, qf4, q4, q16), (kf, kf4, k4, k16), (vf, vf4, v4, v16)):
        for r in range(4):
            x = src32[pl.ds(r, L4, stride=4), :]
            mid32[pl.ds(r * L4, L4), :] = x
            dst4[pl.ds(r * L4, L4), :] = x.astype(BF16)
        for r4 in range(4):
            for a in range(4):
                dst16[pl.ds((4 * a + r4) * L16, L16), :] = (
                    mid32[pl.ds(r4 * L4 + a, L16, stride=4), :].astype(BF16))

    per_class = L4 // TQ
    n_groups = SEQ // TQ // BLOCKS_PER_PATTERN

    def make_blocks(it):
        blocks = []
        for g in range(BLOCKS_PER_PATTERN):
            i = it * BLOCKS_PER_PATTERN + g
            q0, w0, bias = banded_block(aligned(i * TQ, TQ), 0, SEQ)
            blocks.append(dict(q=qn, k=kn, v=vn, q0=q0, w0=w0, bias=bias, width=wide, ones=ones_s,
                               plane=0, rows=pl.ds(q0, TQ)))
            r, j = i // per_class, i % per_class
            q0, w0, bias = banded_block(aligned(j * TQ, TQ), aligned(r * L4, L4), L4)
            blocks.append(dict(q=q4, k=k4, v=v4, q0=q0, w0=w0, bias=bias, width=wide, ones=ones_s,
                               plane=1, rows=pl.ds(r + 4 * j * TQ, TQ, stride=4)))
            base = aligned(i * TQ, TQ)
            blocks.append(dict(q=q16, k=k16, v=v16, q0=base, w0=base, bias=lambda: biasn_s[...],
                               width=TQ, ones=ones_n, plane=2, rows=pl.ds(i, TQ, stride=16)))
        return blocks

    for it in range(n_groups):
        for b in make_blocks(it):
            attend(b)

    def merge_body(i, carry):
        rows = pl.ds(aligned(i * TQ, TQ), TQ)
        m0, m1, m2 = m_s[0, rows, :], m_s[1, rows, :], m_s[2, rows, :]
        mx = jnp.maximum(jnp.maximum(m0, m1), m2)
        e0, e1, e2 = jnp.exp2(m0 - mx), jnp.exp2(m1 - mx), jnp.exp2(m2 - mx)
        num = e0 * acc_s[0, rows, :] + e1 * acc_s[1, rows, :] + e2 * acc_s[2, rows, :]
        den = e0 * l_s[0, rows, :] + e1 * l_s[1, rows, :] + e2 * l_s[2, rows, :]
        o_ref[0, rows, :] = (num / den * g_ref[0, rows, :].astype(F32)).astype(BF16)
        return carry

    lax.fori_loop(0, SEQ // TQ, merge_body, 0)


def _attention(proj):
    B = proj.shape[0]
    nb = A_WIDTH // LANES

    def col(base):
        return pl.BlockSpec((1, SEQ, LANES), lambda b, g: (b, 0, base // LANES + g))

    return pl.pallas_call(
        _attn_kernel,
        grid=(B, nb),
        in_specs=[col(COL_QA), col(COL_KA), col(COL_VA), col(COL_GA)],
        out_specs=pl.BlockSpec((1, SEQ, LANES), lambda b, g: (b, 0, g)),
        out_shape=jax.ShapeDtypeStruct((B, SEQ, A_WIDTH), BF16),
        scratch_shapes=[pltpu.VMEM((SEQ, LANES), F32)] * 6
        + [pltpu.VMEM((SEQ, LANES), BF16)] * 6
        + [pltpu.VMEM((3, TQ, 2 * TQ), F32), pltpu.VMEM((TQ, TQ), F32),
           pltpu.VMEM((4 * TQ, LANES), BF16)]
        + [pltpu.VMEM((3, SEQ, LANES), F32)] * 3,
        compiler_params=pltpu.CompilerParams(
            dimension_semantics=("parallel", "parallel"), vmem_limit_bytes=VMEM_LIMIT),
        name="dilated_attn",
    )(proj, proj, proj, proj)


KDEC_F, KDEC_B, QDEC_F, QDEC_B, DMAT0 = 0, 1, 2, 3, 4


def _ret_kernel(lgf_ref, lgb_ref, q_ref, k_ref, v_ref, g_ref, o_ref,
                tab_s, kv_s, st_ref, s_s, lhs_s):
    gp = pl.program_id(1)
    lf0, lf1 = lgf_ref[2 * gp], lgf_ref[2 * gp + 1]
    lb0, lb1 = lgb_ref[2 * gp], lgb_ref[2 * gp + 1]

    row = lax.broadcasted_iota(jnp.int32, (CHUNK, LANES), 0)
    lane = lax.broadcasted_iota(jnp.int32, (CHUNK, LANES), 1)
    t = row.astype(F32)
    lf = jnp.where(lane < HEAD_DIM, lf0, lf1)
    lb = jnp.where(lane < HEAD_DIM, lb0, lb1)
    diff = (row - lane).astype(F32)
    tab_s[KDEC_F] = jnp.exp((CHUNK - 1.0 - t) * lf)
    tab_s[KDEC_B] = jnp.exp(t * lb)
    tab_s[QDEC_F] = jnp.exp((t + 1.0) * lf)
    tab_s[QDEC_B] = jnp.exp((CHUNK - t) * lb)
    tab_s[DMAT0] = jnp.exp(jnp.where(diff >= 0, diff * lf0, -diff * lb0))
    tab_s[DMAT0 + 1] = jnp.exp(jnp.where(diff >= 0, diff * lf1, -diff * lb1))

    def chunk_rows(n):
        start = n * CHUNK if isinstance(n, int) else pl.multiple_of(n * CHUNK, CHUNK)
        return pl.ds(start, CHUNK)

    def kv_body(n, carry):
        rows = chunk_rows(n)
        kf32 = k_ref[0, rows, :].astype(F32)
        kd = jnp.concatenate([(kf32 * tab_s[KDEC_F]).astype(BF16),
                              (kf32 * tab_s[KDEC_B]).astype(BF16)], axis=1)
        kv = lax.dot_general(kd, v_ref[0, rows, :], (((0,), (0,)), ((), ())),
                             preferred_element_type=F32)
        for slot in range(4):
            h = slot % 2
            kv_s[n, slot] = kv[slot * HEAD_DIM:(slot + 1) * HEAD_DIM, h * LANES:(h + 1) * LANES]
        return carry

    for n in range(N_CHUNKS):
        kv_body(n, 0)

    chunk_decay = [jnp.exp(jnp.full((1, LANES), CHUNK * lg, F32)) for lg in (lf0, lf1, lb0, lb1)]

    def scan_body(n, states):
        nb = N_CHUNKS - 1 - n
        new_states = []
        for slot, state in enumerate(states):
            at = n if slot < 2 else nb
            st_ref[at, slot] = state.astype(BF16)
            new_states.append(chunk_decay[slot] * state + kv_s[at, slot])
        return tuple(new_states)

    states = (jnp.zeros((HEAD_DIM, LANES), F32),) * 4
    for n in range(N_CHUNKS):
        states = scan_body(n, states)

    head0 = lax.broadcasted_iota(jnp.int32, (CHUNK, LANES), 1) < HEAD_DIM
    zero_v = jnp.zeros((CHUNK, LANES), BF16)
    zero_s = jnp.zeros((HEAD_DIM, LANES), BF16)

    def product_stage(chunks):
        for n in chunks:
            rows = chunk_rows(n)
            q = q_ref[0, rows, :]
            zq = jnp.zeros_like(q)
            qs = jnp.concatenate([jnp.where(head0, q, zq), jnp.where(head0, zq, q)], axis=0)
            s_s[n] = lax.dot_general(qs, k_ref[0, rows, :], (((1,), (1,)), ((), ())),
                                     preferred_element_type=F32)

    def decay_stage(chunks):
        for n in chunks:
            qf32 = q_ref[0, chunk_rows(n), :].astype(F32)
            lhs_s[n, :, 0:LANES] = (s_s[n, 0:CHUNK, :] * tab_s[DMAT0]).astype(BF16)
            lhs_s[n, :, LANES:2 * LANES] = (s_s[n, CHUNK:, :] * tab_s[DMAT0 + 1]).astype(BF16)
            lhs_s[n, :, 2 * LANES:3 * LANES] = (qf32 * tab_s[QDEC_F]).astype(BF16)
            lhs_s[n, :, 3 * LANES:] = (qf32 * tab_s[QDEC_B]).astype(BF16)

    def output_stage(chunks):
        for n in chunks:
            rows = chunk_rows(n)
            v = v_ref[0, rows, :]
            left = jnp.concatenate([v[:, :LANES], zero_v, st_ref[n, 0], zero_s, st_ref[n, 2], zero_s], axis=0)
            right = jnp.concatenate([zero_v, v[:, LANES:], zero_s, st_ref[n, 1], zero_s, st_ref[n, 3]], axis=0)
            y = jnp.dot(lhs_s[n], jnp.concatenate([left, right], axis=1),
                        preferred_element_type=F32)
            for h in range(2):
                cols = slice(h * LANES, (h + 1) * LANES)
                r = y[:, cols]
                r = r * lax.rsqrt(jnp.mean(r * r, axis=-1, keepdims=True) + EPS)
                o_ref[0, rows, cols] = (r * g_ref[0, rows, cols].astype(F32)).astype(BF16)

    first, second = range(N_CHUNKS // 2), range(N_CHUNKS // 2, N_CHUNKS)
    product_stage(first)
    decay_stage(first)
    product_stage(second)
    output_stage(first)
    decay_stage(second)
    output_stage(second)


def _retention(proj, lg_f, lg_b):
    B = proj.shape[0]
    smem = pl.BlockSpec(memory_space=pltpu.SMEM)
    return pl.pallas_call(
        _ret_kernel,
        grid=(B, R_HEADS // 2),
        in_specs=[
            smem, smem,
            pl.BlockSpec((1, SEQ, LANES), lambda b, g: (b, 0, COL_QB // LANES + g)),
            pl.BlockSpec((1, SEQ, LANES), lambda b, g: (b, 0, COL_KB // LANES + g)),
            pl.BlockSpec((1, SEQ, 2 * LANES), lambda b, g: (b, 0, COL_VB // (2 * LANES) + g)),
            pl.BlockSpec((1, SEQ, 2 * LANES), lambda b, g: (b, 0, COL_GB // (2 * LANES) + g)),
        ],
        out_specs=pl.BlockSpec((1, SEQ, 2 * LANES), lambda b, g: (b, 0, g)),
        out_shape=jax.ShapeDtypeStruct((B, SEQ, R_WIDTH), BF16),
        scratch_shapes=[pltpu.VMEM((6, CHUNK, LANES), F32),
                        pltpu.VMEM((N_CHUNKS, 4, HEAD_DIM, LANES), F32),
                        pltpu.VMEM((N_CHUNKS, 4, HEAD_DIM, LANES), BF16),
                        pltpu.VMEM((N_CHUNKS, 2 * CHUNK, LANES), F32),
                        pltpu.VMEM((N_CHUNKS, CHUNK, 4 * LANES), BF16)],
        compiler_params=pltpu.CompilerParams(
            dimension_semantics=("parallel", "parallel"), vmem_limit_bytes=VMEM_LIMIT),
        name="retention",
    )(lg_f, lg_b, proj, proj, proj, proj)


def _outproj_kernel(ya_ref, yr_ref, x_ref, gate_ref, w_ref, gfin_ref, o_ref, *, final):
    mix = jnp.concatenate([ya_ref[0], yr_ref[0]], axis=1)
    out = jnp.dot(mix, w_ref[...], preferred_element_type=F32)
    xn = x_ref[0] + gate_ref[0] * out
    if final:
        ms = jnp.mean(xn * xn, axis=-1, keepdims=True)
        xn = xn * lax.rsqrt(ms + EPS) * gfin_ref[...]
    o_ref[0] = xn


def _out_projection(ya, yr, x, mod3, w_bf16, g_final, final):
    B = x.shape[0]
    tm = ROW_TILE
    return pl.pallas_call(
        functools.partial(_outproj_kernel, final=final),
        grid=(B, SEQ // tm),
        in_specs=[
            pl.BlockSpec((1, tm, A_WIDTH), lambda b, i: (b, i, 0)),
            pl.BlockSpec((1, tm, R_WIDTH), lambda b, i: (b, i, 0)),
            pl.BlockSpec((1, tm, D_MODEL), lambda b, i: (b, i, 0)),
            pl.BlockSpec((1, 1, D_MODEL), lambda b, i: (b, 0, 2)),
            pl.BlockSpec((MIX_WIDTH, D_MODEL), lambda b, i: (0, 0)),
            pl.BlockSpec((1, D_MODEL), lambda b, i: (0, 0)),
        ],
        out_specs=pl.BlockSpec((1, tm, D_MODEL), lambda b, i: (b, i, 0)),
        out_shape=jax.ShapeDtypeStruct((B, SEQ, D_MODEL), F32),
        compiler_params=pltpu.CompilerParams(
            dimension_semantics=("parallel", "parallel"), vmem_limit_bytes=VMEM_LIMIT),
        name="out_proj",
    )(ya, yr, x, mod3, w_bf16, g_final)


def _rope_tables():
    inv = ROPE_THETA ** (-jnp.arange(0, HEAD_DIM, 2, dtype=F32) / HEAD_DIM)
    ang = jnp.arange(SEQ, dtype=F32)[:, None] * inv[None, :]
    cos, sin = jnp.cos(ang), jnp.sin(ang)
    return jnp.tile(cos, (1, 4)), jnp.concatenate([-sin, sin, -sin, sin], axis=1)


def _trunk(x, c, g_norm, w_ada, b_ada, w_in_bf16, w_out_bf16, lg_f, lg_b, g_final, tables):
    B = x.shape[0]
    mod = _modulation(c, w_ada, b_ada)
    cos_t, sin_t = tables
    for l in range(DEPTH):
        mod3 = mod[l].reshape(B, 1, 3 * D_MODEL)
        proj = _in_projection(x, mod3, g_norm[l].reshape(1, D_MODEL), w_in_bf16[l], cos_t, sin_t)
        ya = _attention(proj)
        yr = _retention(proj, lg_f[l], lg_b[l])
        x = _out_projection(ya, yr, x, mod3, w_out_bf16[l], g_final.reshape(1, D_MODEL),
                            final=(l == DEPTH - 1))
    return x


def kernel(x_prompt, x_sample, c_prompt, c_sample, g_norm, w_ada, b_ada, w_in, w_out,
           decay_fwd, decay_bwd, g_final):
    w_in_bf16 = w_in.astype(BF16)
    w_out_bf16 = w_out.astype(BF16)
    lg_f = jax.nn.log_sigmoid(decay_fwd.astype(F32))
    lg_b = jax.nn.log_sigmoid(decay_bwd.astype(F32))
    tables = _rope_tables()
    args = (g_norm, w_ada, b_ada, w_in_bf16, w_out_bf16, lg_f, lg_b, g_final, tables)
    return (_trunk(x_prompt, c_prompt, *args), _trunk(x_sample, c_sample, *args))
```

```python
import functools
import math

import jax
import jax.numpy as jnp
from jax import lax
from jax.experimental import pallas as pl
from jax.experimental.pallas import tpu as pltpu

F32 = jnp.float32
BF16 = jnp.bfloat16

D_MODEL = 1024
SEQ = 2048
DEPTH = 2
HEAD_DIM = 64
A_WIDTH = 512
R_HEADS = 4
R_WIDTH = 512
MIX_WIDTH = 1024
IN_WIDTH = 3584
CHUNK = 128
N_CHUNKS = SEQ // CHUNK
ROPE_THETA = 10000.0
EPS = 1e-6
NEG = -1e30
RADIUS = 64
LOG2E = math.log2(math.e)

LANES = 128
TQ = 128
ROW_TILE = 512
OUT_ROW_TILE = 1024
COL_CHUNK = 512
VMEM_LIMIT = 48 * 1024 * 1024

COL_QA, COL_KA, COL_VA, COL_GA = 0, 512, 1024, 1536
COL_QB, COL_KB, COL_VB, COL_GB = 2048, 2304, 2560, 3072


def _sigmoid(x):
    return 1.0 / (1.0 + jnp.exp(-x))


def _mod_kernel(c_ref, w_ref, b_ref, o_ref):
    c = c_ref[...]
    a = c * _sigmoid(c)
    o_ref[0] = jnp.dot(a, w_ref[0], preferred_element_type=F32,
                       precision=lax.Precision.HIGHEST) + b_ref[0]


def _modulation(c, w_ada, b_ada):
    B = c.shape[0]
    nt = 512
    return pl.pallas_call(
        _mod_kernel,
        grid=(DEPTH, 3 * D_MODEL // nt),
        in_specs=[
            pl.BlockSpec((B, D_MODEL), lambda l, j: (0, 0)),
            pl.BlockSpec((1, D_MODEL, nt), lambda l, j: (l, 0, j)),
            pl.BlockSpec((1, 1, nt), lambda l, j: (l, 0, j)),
        ],
        out_specs=pl.BlockSpec((1, B, nt), lambda l, j: (l, 0, j)),
        out_shape=jax.ShapeDtypeStruct((DEPTH, B, 3 * D_MODEL), F32),
        name="adaln_mod",
    )(c, w_ada, b_ada.reshape(DEPTH, 1, 3 * D_MODEL))


def _inproj_kernel(x_ref, shift_ref, scale_ref, g_ref, w_ref, cos_ref, sin_ref, o_ref):
    x = x_ref[0]
    ms = jnp.mean(x * x, axis=-1, keepdims=True)
    h = x * lax.rsqrt(ms + EPS) * g_ref[...]
    h = h * (1.0 + scale_ref[0]) + shift_ref[0]
    hb = h.astype(BF16)
    cos = cos_ref[...]
    sin = sin_ref[...]
    lane = lax.broadcasted_iota(jnp.int32, cos.shape, 1)
    first_half = (lane & 32) == 0
    for c in range(IN_WIDTH // COL_CHUNK):
        acc = jnp.dot(hb, w_ref[:, c * COL_CHUNK:(c + 1) * COL_CHUNK],
                      preferred_element_type=F32)
        for j in range(COL_CHUNK // LANES):
            col = c * COL_CHUNK + j * LANES
            a = acc[:, j * LANES:(j + 1) * LANES]
            if col < COL_VA or COL_QB <= col < COL_VB:
                partner = jnp.where(first_half, pltpu.roll(a, 96, 1), pltpu.roll(a, 32, 1))
                a = a * cos + partner * sin
            if col < COL_KA:
                a = a * (HEAD_DIM ** -0.5 * LOG2E)
            if COL_KB <= col < COL_VB:
                a = a * (HEAD_DIM ** -0.5)
            if COL_GA <= col < COL_QB or col >= COL_GB:
                a = a * _sigmoid(a)
            o_ref[0, :, col:col + LANES] = a.astype(BF16)


def _in_projection(x, mod3, g, w_bf16, cos_t, sin_t):
    B = x.shape[0]
    tm = ROW_TILE
    return pl.pallas_call(
        _inproj_kernel,
        grid=(B, SEQ // tm),
        in_specs=[
            pl.BlockSpec((1, tm, D_MODEL), lambda b, i: (b, i, 0)),
            pl.BlockSpec((1, 1, D_MODEL), lambda b, i: (b, 0, 0)),
            pl.BlockSpec((1, 1, D_MODEL), lambda b, i: (b, 0, 1)),
            pl.BlockSpec((1, D_MODEL), lambda b, i: (0, 0)),
            pl.BlockSpec((D_MODEL, IN_WIDTH), lambda b, i: (0, 0)),
            pl.BlockSpec((tm, LANES), lambda b, i: (i, 0)),
            pl.BlockSpec((tm, LANES), lambda b, i: (i, 0)),
        ],
        out_specs=pl.BlockSpec((1, tm, IN_WIDTH), lambda b, i: (b, i, 0)),
        out_shape=jax.ShapeDtypeStruct((B, SEQ, IN_WIDTH), BF16),
        compiler_params=pltpu.CompilerParams(
            dimension_semantics=("parallel", "parallel"), vmem_limit_bytes=VMEM_LIMIT),
        name="in_proj",
    )(x, mod3, mod3, g, w_bf16, cos_t, sin_t)


def _band_bias(off, width):
    r = lax.broadcasted_iota(jnp.int32, (TQ, width), 0)
    c = lax.broadcasted_iota(jnp.int32, (TQ, width), 1)
    d = r + off - c
    return jnp.where((d <= RADIUS) & (d >= -RADIUS), 0.0, NEG).astype(F32)


def _head_ones(width):
    r = lax.broadcasted_iota(jnp.int32, (2 * width, LANES), 0)
    c = lax.broadcasted_iota(jnp.int32, (2 * width, LANES), 1)
    return jnp.where((r < width) == (c < HEAD_DIM), 1.0, 0.0).astype(BF16)


def _attn_kernel(q_ref, k_ref, v_ref, g_ref, o_ref,
                 qf, kf, vf, qf4, kf4, vf4, q4, k4, v4, q16, k16, v16,
                 bias_s, biasn_s, ones_s, acc_s, m_s, l_s):
    qn, kn, vn = q_ref.at[0], k_ref.at[0], v_ref.at[0]
    wide = 2 * TQ

    def attend(b):
        width = b["width"]
        q = b["q"][pl.ds(b["q0"], TQ), :]
        head0 = lax.broadcasted_iota(jnp.int32, q.shape, 1) < HEAD_DIM
        zq = jnp.zeros_like(q)
        qs = jnp.concatenate([jnp.where(head0, q, zq), jnp.where(head0, zq, q)], axis=0)
        kw = b["k"][pl.ds(b["w0"], width), :]
        s = lax.dot_general(qs, kw, (((1,), (1,)), ((), ())), preferred_element_type=F32)
        bias = b["bias"]()
        s = s + jnp.concatenate([bias, bias], axis=0)
        m = jnp.max(s, axis=-1, keepdims=True)
        p = jnp.exp2(s - m).astype(BF16)
        pc = jnp.concatenate([p[:TQ], p[TQ:]], axis=1)
        vw = b["v"][pl.ds(b["w0"], width), :]
        vhead0 = lax.broadcasted_iota(jnp.int32, vw.shape, 1) < HEAD_DIM
        zv = jnp.zeros_like(vw)
        vs = jnp.concatenate([jnp.where(vhead0, vw, zv), jnp.where(vhead0, zv, vw)], axis=0)
        r = jnp.dot(pc, jnp.concatenate([vs, b["ones"][...]], axis=1),
                    preferred_element_type=F32)
        h0f = lax.broadcasted_iota(jnp.int32, (TQ, LANES), 1) < HEAD_DIM
        mexp = jnp.where(h0f, m[:TQ], m[TQ:])
        if b["plane"] is None:
            merge(b["rows"], mexp, r[:, :LANES], r[:, LANES:])
        else:
            m_s[b["plane"], b["rows"], :] = mexp
            acc_s[b["plane"], b["rows"], :] = r[:, :LANES]
            l_s[b["plane"], b["rows"], :] = r[:, LANES:]

    def merge(rows, m0, acc0, l0):
        m1, m2 = m_s[0, rows, :], m_s[1, rows, :]
        mx = jnp.maximum(jnp.maximum(m0, m1), m2)
        e0, e1, e2 = jnp.exp2(m0 - mx), jnp.exp2(m1 - mx), jnp.exp2(m2 - mx)
        num = e0 * acc0 + e1 * acc_s[0, rows, :] + e2 * acc_s[1, rows, :]
        den = e0 * l0 + e1 * l_s[0, rows, :] + e2 * l_s[1, rows, :]
        o_ref[0, rows, :] = (num / den * g_ref[0, rows, :].astype(F32)).astype(BF16)

    def banded_block(q_local, base, length):
        w_local = min(max(q_local - RADIUS, 0), length - wide)
        kind = (q_local - w_local) // RADIUS
        return base + q_local, base + w_local, lambda: bias_s[kind]

    for kind in range(3):
        bias_s[kind] = _band_bias(kind * RADIUS, wide)
    biasn_s[...] = _band_bias(0, TQ)
    ones_s[...] = _head_ones(wide)
    ones_n = ones_s.at[pl.ds(TQ, wide)]

    qf[...] = qn[...].astype(F32)
    kf[...] = kn[...].astype(F32)
    vf[...] = vn[...].astype(F32)
    L4, L16 = SEQ // 4, SEQ // 16
    for src32, mid32, dst4, dst16 in ((qf, qf4, q4, q16), (kf, kf4, k4, k16), (vf, vf4, v4, v16)):
        for r in range(4):
            x = src32[pl.ds(r, L4, stride=4), :]
            mid32[pl.ds(r * L4, L4), :] = x
            dst4[pl.ds(r * L4, L4), :] = x.astype(BF16)
        for r4 in range(4):
            for a in range(4):
                dst16[pl.ds((4 * a + r4) * L16, L16), :] = (
                    mid32[pl.ds(r4 * L4 + a, L16, stride=4), :].astype(BF16))

    per_class = L4 // TQ

    for i in range(SEQ // TQ):
        r, j = i // per_class, i % per_class
        q0, w0, bias = banded_block(j * TQ, r * L4, L4)
        attend(dict(q=q4, k=k4, v=v4, q0=q0, w0=w0, bias=bias, width=wide, ones=ones_s,
                    plane=0, rows=pl.ds(r + 4 * j * TQ, TQ, stride=4)))
        attend(dict(q=q16, k=k16, v=v16, q0=i * TQ, w0=i * TQ, bias=lambda: biasn_s[...],
                    width=TQ, ones=ones_n, plane=1, rows=pl.ds(i, TQ, stride=16)))
    for i in range(SEQ // TQ):
        q0, w0, bias = banded_block(i * TQ, 0, SEQ)
        attend(dict(q=qn, k=kn, v=vn, q0=q0, w0=w0, bias=bias, width=wide, ones=ones_s,
                    plane=None, rows=pl.ds(q0, TQ)))


def _attention(proj):
    B = proj.shape[0]
    nb = A_WIDTH // LANES

    def col(base):
        return pl.BlockSpec((1, SEQ, LANES), lambda b, g: (b, 0, base // LANES + g))

    return pl.pallas_call(
        _attn_kernel,
        grid=(B, nb),
        in_specs=[col(COL_QA), col(COL_KA), col(COL_VA), col(COL_GA)],
        out_specs=pl.BlockSpec((1, SEQ, LANES), lambda b, g: (b, 0, g)),
        out_shape=jax.ShapeDtypeStruct((B, SEQ, A_WIDTH), BF16),
        scratch_shapes=[pltpu.VMEM((SEQ, LANES), F32)] * 6
        + [pltpu.VMEM((SEQ, LANES), BF16)] * 6
        + [pltpu.VMEM((3, TQ, 2 * TQ), F32), pltpu.VMEM((TQ, TQ), F32),
           pltpu.VMEM((4 * TQ, LANES), BF16)]
        + [pltpu.VMEM((2, SEQ, LANES), F32)] * 3,
        compiler_params=pltpu.CompilerParams(
            dimension_semantics=("parallel", "parallel"), vmem_limit_bytes=VMEM_LIMIT),
        name="dilated_attn",
    )(proj, proj, proj, proj)


KDEC_F, KDEC_B, QDEC_F, QDEC_B, DMAT0 = 0, 1, 2, 3, 4


def _ret_kernel(lgf_ref, lgb_ref, q_ref, k_ref, v_ref, g_ref, o_ref,
                tab_s, kv_s, st_ref, s_s, lhs_s):
    gp = pl.program_id(1)
    lf0, lf1 = lgf_ref[2 * gp], lgf_ref[2 * gp + 1]
    lb0, lb1 = lgb_ref[2 * gp], lgb_ref[2 * gp + 1]

    row = lax.broadcasted_iota(jnp.int32, (CHUNK, LANES), 0)
    lane = lax.broadcasted_iota(jnp.int32, (CHUNK, LANES), 1)
    t = row.astype(F32)
    lf = jnp.where(lane < HEAD_DIM, lf0, lf1)
    lb = jnp.where(lane < HEAD_DIM, lb0, lb1)
    diff = (row - lane).astype(F32)
    tab_s[KDEC_F] = jnp.exp((CHUNK - 1.0 - t) * lf)
    tab_s[KDEC_B] = jnp.exp(t * lb)
    tab_s[QDEC_F] = jnp.exp((t + 1.0) * lf)
    tab_s[QDEC_B] = jnp.exp((CHUNK - t) * lb)
    tab_s[DMAT0] = jnp.exp(jnp.where(diff >= 0, diff * lf0, -diff * lb0))
    tab_s[DMAT0 + 1] = jnp.exp(jnp.where(diff >= 0, diff * lf1, -diff * lb1))

    def chunk_rows(n):
        start = n * CHUNK if isinstance(n, int) else pl.multiple_of(n * CHUNK, CHUNK)
        return pl.ds(start, CHUNK)

    def kv_body(n, carry):
        rows = chunk_rows(n)
        kf32 = k_ref[0, rows, :].astype(F32)
        kd = jnp.concatenate([(kf32 * tab_s[KDEC_F]).astype(BF16),
                              (kf32 * tab_s[KDEC_B]).astype(BF16)], axis=1)
        kv = lax.dot_general(kd, v_ref[0, rows, :], (((0,), (0,)), ((), ())),
                             preferred_element_type=F32)
        for slot in range(4):
            h = slot % 2
            kv_s[n, slot] = kv[slot * HEAD_DIM:(slot + 1) * HEAD_DIM, h * LANES:(h + 1) * LANES]
        return carry

    for n in range(N_CHUNKS):
        kv_body(n, 0)

    chunk_decay = [jnp.exp(jnp.full((1, LANES), CHUNK * lg, F32)) for lg in (lf0, lf1, lb0, lb1)]

    def scan_body(n, states):
        nb = N_CHUNKS - 1 - n
        new_states = []
        for slot, state in enumerate(states):
            at = n if slot < 2 else nb
            st_ref[at, slot] = state.astype(BF16)
            new_states.append(chunk_decay[slot] * state + kv_s[at, slot])
        return tuple(new_states)

    states = (jnp.zeros((HEAD_DIM, LANES), F32),) * 4
    for n in range(N_CHUNKS):
        states = scan_body(n, states)

    head0 = lax.broadcasted_iota(jnp.int32, (CHUNK, LANES), 1) < HEAD_DIM
    zero_v = jnp.zeros((CHUNK, LANES), BF16)
    zero_s = jnp.zeros((HEAD_DIM, LANES), BF16)

    def product_stage(chunks):
        for n in chunks:
            rows = chunk_rows(n)
            q = q_ref[0, rows, :]
            zq = jnp.zeros_like(q)
            qs = jnp.concatenate([jnp.where(head0, q, zq), jnp.where(head0, zq, q)], axis=0)
            s_s[n] = lax.dot_general(qs, k_ref[0, rows, :], (((1,), (1,)), ((), ())),
                                     preferred_element_type=F32)

    def decay_stage(chunks):
        for n in chunks:
            qf32 = q_ref[0, chunk_rows(n), :].astype(F32)
            lhs_s[n, :, 0:LANES] = (s_s[n, 0:CHUNK, :] * tab_s[DMAT0]).astype(BF16)
            lhs_s[n, :, LANES:2 * LANES] = (s_s[n, CHUNK:, :] * tab_s[DMAT0 + 1]).astype(BF16)
            lhs_s[n, :, 2 * LANES:3 * LANES] = (qf32 * tab_s[QDEC_F]).astype(BF16)
            lhs_s[n, :, 3 * LANES:] = (qf32 * tab_s[QDEC_B]).astype(BF16)

    def output_stage(chunks):
        for n in chunks:
            rows = chunk_rows(n)
            v = v_ref[0, rows, :]
            left = jnp.concatenate([v[:, :LANES], zero_v, st_ref[n, 0], zero_s, st_ref[n, 2], zero_s], axis=0)
            right = jnp.concatenate([zero_v, v[:, LANES:], zero_s, st_ref[n, 1], zero_s, st_ref[n, 3]], axis=0)
            y = jnp.dot(lhs_s[n], jnp.concatenate([left, right], axis=1),
                        preferred_element_type=F32)
            for h in range(2):
                cols = slice(h * LANES, (h + 1) * LANES)
                r = y[:, cols]
                r = r * lax.rsqrt(jnp.mean(r * r, axis=-1, keepdims=True) + EPS)
                o_ref[0, rows, cols] = (r * g_ref[0, rows, cols].astype(F32)).astype(BF16)

    first, second = range(N_CHUNKS // 2), range(N_CHUNKS // 2, N_CHUNKS)
    product_stage(first)
    decay_stage(first)
    product_stage(second)
    output_stage(first)
    decay_stage(second)
    output_stage(second)


def _retention(proj, lg_f, lg_b):
    B = proj.shape[0]
    smem = pl.BlockSpec(memory_space=pltpu.SMEM)
    return pl.pallas_call(
        _ret_kernel,
        grid=(B, R_HEADS // 2),
        in_specs=[
            smem, smem,
            pl.BlockSpec((1, SEQ, LANES), lambda b, g: (b, 0, COL_QB // LANES + g)),
            pl.BlockSpec((1, SEQ, LANES), lambda b, g: (b, 0, COL_KB // LANES + g)),
            pl.BlockSpec((1, SEQ, 2 * LANES), lambda b, g: (b, 0, COL_VB // (2 * LANES) + g)),
            pl.BlockSpec((1, SEQ, 2 * LANES), lambda b, g: (b, 0, COL_GB // (2 * LANES) + g)),
        ],
        out_specs=pl.BlockSpec((1, SEQ, 2 * LANES), lambda b, g: (b, 0, g)),
        out_shape=jax.ShapeDtypeStruct((B, SEQ, R_WIDTH), BF16),
        scratch_shapes=[pltpu.VMEM((6, CHUNK, LANES), F32),
                        pltpu.VMEM((N_CHUNKS, 4, HEAD_DIM, LANES), F32),
                        pltpu.VMEM((N_CHUNKS, 4, HEAD_DIM, LANES), BF16),
                        pltpu.VMEM((N_CHUNKS, 2 * CHUNK, LANES), F32),
                        pltpu.VMEM((N_CHUNKS, CHUNK, 4 * LANES), BF16)],
        compiler_params=pltpu.CompilerParams(
            dimension_semantics=("parallel", "parallel"), vmem_limit_bytes=VMEM_LIMIT),
        name="retention",
    )(lg_f, lg_b, proj, proj, proj, proj)


def _outproj_kernel(ya_ref, yr_ref, x_ref, gate_ref, w_ref, gfin_ref, o_ref, *, final):
    mix = jnp.concatenate([ya_ref[0], yr_ref[0]], axis=1)
    out = jnp.dot(mix, w_ref[...], preferred_element_type=F32)
    xn = x_ref[0] + gate_ref[0] * out
    if final:
        ms = jnp.mean(xn * xn, axis=-1, keepdims=True)
        xn = xn * lax.rsqrt(ms + EPS) * gfin_ref[...]
    o_ref[0] = xn


def _out_projection(ya, yr, x, mod3, w_bf16, g_final, final):
    B = x.shape[0]
    tm = OUT_ROW_TILE
    return pl.pallas_call(
        functools.partial(_outproj_kernel, final=final),
        grid=(B, SEQ // tm),
        in_specs=[
            pl.BlockSpec((1, tm, A_WIDTH), lambda b, i: (b, i, 0)),
            pl.BlockSpec((1, tm, R_WIDTH), lambda b, i: (b, i, 0)),
            pl.BlockSpec((1, tm, D_MODEL), lambda b, i: (b, i, 0)),
            pl.BlockSpec((1, 1, D_MODEL), lambda b, i: (b, 0, 2)),
            pl.BlockSpec((MIX_WIDTH, D_MODEL), lambda b, i: (0, 0)),
            pl.BlockSpec((1, D_MODEL), lambda b, i: (0, 0)),
        ],
        out_specs=pl.BlockSpec((1, tm, D_MODEL), lambda b, i: (b, i, 0)),
        out_shape=jax.ShapeDtypeStruct((B, SEQ, D_MODEL), F32),
        compiler_params=pltpu.CompilerParams(
            dimension_semantics=("parallel", "parallel"), vmem_limit_bytes=VMEM_LIMIT),
        name="out_proj",
    )(ya, yr, x, mod3, w_bf16, g_final)


def _rope_tables():
    inv = ROPE_THETA ** (-jnp.arange(0, HEAD_DIM, 2, dtype=F32) / HEAD_DIM)
    ang = jnp.arange(SEQ, dtype=F32)[:, None] * inv[None, :]
    cos, sin = jnp.cos(ang), jnp.sin(ang)
    return jnp.tile(cos, (1, 4)), jnp.concatenate([-sin, sin, -sin, sin], axis=1)


def _trunk(x, c, g_norm, w_ada, b_ada, w_in_bf16, w_out_bf16, lg_f, lg_b, g_final, tables):
    B = x.shape[0]
    mod = _modulation(c, w_ada, b_ada)
    cos_t, sin_t = tables
    for l in range(DEPTH):
        mod3 = mod[l].reshape(B, 1, 3 * D_MODEL)
        proj = _in_projection(x, mod3, g_norm[l].reshape(1, D_MODEL), w_in_bf16[l], cos_t, sin_t)
        ya = _attention(proj)
        yr = _retention(proj, lg_f[l], lg_b[l])
        x = _out_projection(ya, yr, x, mod3, w_out_bf16[l], g_final.reshape(1, D_MODEL),
                            final=(l == DEPTH - 1))
    return x


def kernel(x_prompt, x_sample, c_prompt, c_sample, g_norm, w_ada, b_ada, w_in, w_out,
           decay_fwd, decay_bwd, g_final):
    w_in_bf16 = w_in.astype(BF16)
    w_out_bf16 = w_out.astype(BF16)
    lg_f = jax.nn.log_sigmoid(decay_fwd.astype(F32))
    lg_b = jax.nn.log_sigmoid(decay_bwd.astype(F32))
    tables = _rope_tables()
    args = (g_norm, w_ada, b_ada, w_in_bf16, w_out_bf16, lg_f, lg_b, g_final, tables)
    return (_trunk(x_prompt, c_prompt, *args), _trunk(x_sample, c_sample, *args))
```

```python
import functools
import math

import jax
import jax.numpy as jnp
from jax import lax
from jax.experimental import pallas as pl
from jax.experimental.pallas import tpu as pltpu

F32 = jnp.float32
BF16 = jnp.bfloat16

D_MODEL = 1024
SEQ = 2048
DEPTH = 2
HEAD_DIM = 64
A_WIDTH = 512
R_HEADS = 4
R_WIDTH = 512
MIX_WIDTH = 1024
IN_WIDTH = 3584
CHUNK = 128
N_CHUNKS = SEQ // CHUNK
ROPE_THETA = 10000.0
EPS = 1e-6
NEG = -1e30
RADIUS = 64
LOG2E = math.log2(math.e)

LANES = 128
TQ = 128
CLASS16_PITCH = 24
ROW_TILE = 512
OUT_ROW_TILE = 1024
COL_CHUNK = 512
VMEM_LIMIT = 48 * 1024 * 1024

COL_QA, COL_KA, COL_VA, COL_GA = 0, 512, 1024, 1536
COL_QB, COL_KB, COL_VB, COL_GB = 2048, 2304, 2560, 3072


def _sigmoid(x):
    return 1.0 / (1.0 + jnp.exp(-x))


def _mod_kernel(c_ref, w_ref, b_ref, o_ref):
    c = c_ref[...]
    a = c * _sigmoid(c)
    o_ref[0] = jnp.dot(a, w_ref[0], preferred_element_type=F32,
                       precision=lax.Precision.HIGHEST) + b_ref[0]


def _modulation(c, w_ada, b_ada):
    B = c.shape[0]
    nt = 512
    return pl.pallas_call(
        _mod_kernel,
        grid=(DEPTH, 3 * D_MODEL // nt),
        in_specs=[
            pl.BlockSpec((B, D_MODEL), lambda l, j: (0, 0)),
            pl.BlockSpec((1, D_MODEL, nt), lambda l, j: (l, 0, j)),
            pl.BlockSpec((1, 1, nt), lambda l, j: (l, 0, j)),
        ],
        out_specs=pl.BlockSpec((1, B, nt), lambda l, j: (l, 0, j)),
        out_shape=jax.ShapeDtypeStruct((DEPTH, B, 3 * D_MODEL), F32),
        name="adaln_mod",
    )(c, w_ada, b_ada.reshape(DEPTH, 1, 3 * D_MODEL))


def _inproj_kernel(x_ref, shift_ref, scale_ref, g_ref, w_ref, cos_ref, sin_ref, o_ref):
    x = x_ref[0]
    ms = jnp.mean(x * x, axis=-1, keepdims=True)
    h = x * lax.rsqrt(ms + EPS) * g_ref[...]
    h = h * (1.0 + scale_ref[0]) + shift_ref[0]
    hb = h.astype(BF16)
    cos = cos_ref[...]
    sin = sin_ref[...]
    lane = lax.broadcasted_iota(jnp.int32, cos.shape, 1)
    first_half = (lane & 32) == 0
    for c in range(IN_WIDTH // COL_CHUNK):
        acc = jnp.dot(hb, w_ref[:, c * COL_CHUNK:(c + 1) * COL_CHUNK],
                      preferred_element_type=F32)
        for j in range(COL_CHUNK // LANES):
            col = c * COL_CHUNK + j * LANES
            a = acc[:, j * LANES:(j + 1) * LANES]
            if col < COL_VA or COL_QB <= col < COL_VB:
                partner = jnp.where(first_half, pltpu.roll(a, 96, 1), pltpu.roll(a, 32, 1))
                a = a * cos + partner * sin
            if col < COL_KA:
                a = a * (HEAD_DIM ** -0.5 * LOG2E)
            if COL_KB <= col < COL_VB:
                a = a * (HEAD_DIM ** -0.5)
            if COL_GA <= col < COL_QB or col >= COL_GB:
                a = a * _sigmoid(a)
            o_ref[0, :, col:col + LANES] = a.astype(BF16)


def _in_projection(x, mod3, g, w_bf16, cos_t, sin_t):
    B = x.shape[0]
    tm = ROW_TILE
    return pl.pallas_call(
        _inproj_kernel,
        grid=(B, SEQ // tm),
        in_specs=[
            pl.BlockSpec((1, tm, D_MODEL), lambda b, i: (b, i, 0)),
            pl.BlockSpec((1, 1, D_MODEL), lambda b, i: (b, 0, 0)),
            pl.BlockSpec((1, 1, D_MODEL), lambda b, i: (b, 0, 1)),
            pl.BlockSpec((1, D_MODEL), lambda b, i: (0, 0)),
            pl.BlockSpec((D_MODEL, IN_WIDTH), lambda b, i: (0, 0)),
            pl.BlockSpec((tm, LANES), lambda b, i: (i, 0)),
            pl.BlockSpec((tm, LANES), lambda b, i: (i, 0)),
        ],
        out_specs=pl.BlockSpec((1, tm, IN_WIDTH), lambda b, i: (b, i, 0)),
        out_shape=jax.ShapeDtypeStruct((B, SEQ, IN_WIDTH), BF16),
        compiler_params=pltpu.CompilerParams(
            dimension_semantics=("parallel", "parallel"), vmem_limit_bytes=VMEM_LIMIT),
        name="in_proj",
    )(x, mod3, mod3, g, w_bf16, cos_t, sin_t)


def _band_bias(off, width):
    r = lax.broadcasted_iota(jnp.int32, (TQ, width), 0)
    c = lax.broadcasted_iota(jnp.int32, (TQ, width), 1)
    d = r + off - c
    return jnp.where((d <= RADIUS) & (d >= -RADIUS), 0.0, NEG).astype(F32)


def _head_ones(width):
    r = lax.broadcasted_iota(jnp.int32, (2 * width, LANES), 0)
    c = lax.broadcasted_iota(jnp.int32, (2 * width, LANES), 1)
    return jnp.where((r < width) == (c < HEAD_DIM), 1.0, 0.0).astype(BF16)


def _attn_kernel(q_ref, k_ref, v_ref, g_ref, o_ref,
                 qf, kf, vf, qf4, kf4, vf4, q4, k4, v4, q16, k16, v16,
                 bias_s, biasn_s, ones_s, m4, acc4, l4, m16, acc16, l16):
    qn, kn, vn = q_ref.at[0], k_ref.at[0], v_ref.at[0]
    wide = 2 * TQ
    planes4, planes16 = (m4, acc4, l4), (m16, acc16, l16)

    def attend(b):
        width = b["width"]
        q = b["q"][pl.ds(b["q0"], TQ), :]
        head0 = lax.broadcasted_iota(jnp.int32, q.shape, 1) < HEAD_DIM
        zq = jnp.zeros_like(q)
        qs = jnp.concatenate([jnp.where(head0, q, zq), jnp.where(head0, zq, q)], axis=0)
        kw = b["k"][pl.ds(b["w0"], width), :]
        s = lax.dot_general(qs, kw, (((1,), (1,)), ((), ())), preferred_element_type=F32)
        bias = b["bias"]()
        s = s + jnp.concatenate([bias, bias], axis=0)
        m = jnp.max(s, axis=-1, keepdims=True)
        p = jnp.exp2(s - m).astype(BF16)
        pc = jnp.concatenate([p[:TQ], p[TQ:]], axis=1)
        vw = b["v"][pl.ds(b["w0"], width), :]
        vhead0 = lax.broadcasted_iota(jnp.int32, vw.shape, 1) < HEAD_DIM
        zv = jnp.zeros_like(vw)
        vs = jnp.concatenate([jnp.where(vhead0, vw, zv), jnp.where(vhead0, zv, vw)], axis=0)
        r = jnp.dot(pc, jnp.concatenate([vs, b["ones"][...]], axis=1),
                    preferred_element_type=F32)
        h0f = lax.broadcasted_iota(jnp.int32, (TQ, LANES), 1) < HEAD_DIM
        mexp = jnp.where(h0f, m[:TQ], m[TQ:])
        if b["planes"] is None:
            merge(b["block"], mexp, r[:, :LANES], r[:, LANES:])
        else:
            for plane, value in zip(b["planes"], (mexp, r[:, :LANES], r[:, LANES:])):
                plane[b["rows"], :] = value

    def class16_rows(plane, i):
        return jnp.concatenate(
            [plane[pl.ds((i * (TQ // 16) + a) * CLASS16_PITCH, 16), :] for a in range(TQ // 16)], axis=0)

    def merge(i, m0, acc0, l0):
        rows = pl.ds(i * TQ, TQ)
        m1, acc1, l1 = (plane[rows, :] for plane in planes4)
        m2, acc2, l2 = (class16_rows(plane, i) for plane in planes16)
        mx = jnp.maximum(jnp.maximum(m0, m1), m2)
        e0, e1, e2 = jnp.exp2(m0 - mx), jnp.exp2(m1 - mx), jnp.exp2(m2 - mx)
        num = e0 * acc0 + e1 * acc1 + e2 * acc2
        den = e0 * l0 + e1 * l1 + e2 * l2
        o_ref[0, rows, :] = (num / den * g_ref[0, rows, :].astype(F32)).astype(BF16)

    def banded_block(q_local, base, length):
        w_local = min(max(q_local - RADIUS, 0), length - wide)
        kind = (q_local - w_local) // RADIUS
        return base + q_local, base + w_local, lambda: bias_s[kind]

    for kind in range(3):
        bias_s[kind] = _band_bias(kind * RADIUS, wide)
    biasn_s[...] = _band_bias(0, TQ)
    ones_s[...] = _head_ones(wide)
    ones_n = ones_s.at[pl.ds(TQ, wide)]

    qf[...] = qn[...].astype(F32)
    kf[...] = kn[...].astype(F32)
    vf[...] = vn[...].astype(F32)
    L4, L16 = SEQ // 4, SEQ // 16
    for src32, mid32, dst4, dst16 in ((qf, qf4, q4, q16), (kf, kf4, k4, k16), (vf, vf4, v4, v16)):
        for r in range(4):
            x = src32[pl.ds(r, L4, stride=4), :]
            mid32[pl.ds(r * L4, L4), :] = x
            dst4[pl.ds(r * L4, L4), :] = x.astype(BF16)
        for r4 in range(4):
            for a in range(4):
                dst16[pl.ds((4 * a + r4) * L16, L16), :] = (
                    mid32[pl.ds(r4 * L4 + a, L16, stride=4), :].astype(BF16))

    per_class = L4 // TQ

    for i in range(SEQ // TQ):
        r, j = i // per_class, i % per_class
        q0, w0, bias = banded_block(j * TQ, r * L4, L4)
        attend(dict(q=q4, k=k4, v=v4, q0=q0, w0=w0, bias=bias, width=wide, ones=ones_s,
                    planes=planes4, rows=pl.ds(r + 4 * j * TQ, TQ, stride=4)))
        attend(dict(q=q16, k=k16, v=v16, q0=i * TQ, w0=i * TQ, bias=lambda: biasn_s[...],
                    width=TQ, ones=ones_n, planes=planes16, rows=pl.ds(i, TQ, stride=CLASS16_PITCH)))
    for i in range(SEQ // TQ):
        q0, w0, bias = banded_block(i * TQ, 0, SEQ)
        attend(dict(q=qn, k=kn, v=vn, q0=q0, w0=w0, bias=bias, width=wide, ones=ones_s,
                    planes=None, block=i))


def _attention(proj):
    B = proj.shape[0]
    nb = A_WIDTH // LANES

    def col(base):
        return pl.BlockSpec((1, SEQ, LANES), lambda b, g: (b, 0, base // LANES + g))

    return pl.pallas_call(
        _attn_kernel,
        grid=(B, nb),
        in_specs=[col(COL_QA), col(COL_KA), col(COL_VA), col(COL_GA)],
        out_specs=pl.BlockSpec((1, SEQ, LANES), lambda b, g: (b, 0, g)),
        out_shape=jax.ShapeDtypeStruct((B, SEQ, A_WIDTH), BF16),
        scratch_shapes=[pltpu.VMEM((SEQ, LANES), F32)] * 6
        + [pltpu.VMEM((SEQ, LANES), BF16)] * 6
        + [pltpu.VMEM((3, TQ, 2 * TQ), F32), pltpu.VMEM((TQ, TQ), F32),
           pltpu.VMEM((4 * TQ, LANES), BF16)]
        + [pltpu.VMEM((SEQ, LANES), F32)] * 3
        + [pltpu.VMEM((SEQ // 16 * CLASS16_PITCH, LANES), F32)] * 3,
        compiler_params=pltpu.CompilerParams(
            dimension_semantics=("parallel", "parallel"), vmem_limit_bytes=VMEM_LIMIT),
        name="dilated_attn",
    )(proj, proj, proj, proj)


KDEC_F, KDEC_B, QDEC_F, QDEC_B, DMAT0 = 0, 1, 2, 3, 4


def _ret_kernel(lgf_ref, lgb_ref, q_ref, k_ref, v_ref, g_ref, o_ref,
                tab_s, kv_s, st_ref, s_s, lhs_s):
    gp = pl.program_id(1)
    lf0, lf1 = lgf_ref[2 * gp], lgf_ref[2 * gp + 1]
    lb0, lb1 = lgb_ref[2 * gp], lgb_ref[2 * gp + 1]

    row = lax.broadcasted_iota(jnp.int32, (CHUNK, LANES), 0)
    lane = lax.broadcasted_iota(jnp.int32, (CHUNK, LANES), 1)
    t = row.astype(F32)
    lf = jnp.where(lane < HEAD_DIM, lf0, lf1)
    lb = jnp.where(lane < HEAD_DIM, lb0, lb1)
    diff = (row - lane).astype(F32)
    tab_s[KDEC_F] = jnp.exp((CHUNK - 1.0 - t) * lf)
    tab_s[KDEC_B] = jnp.exp(t * lb)
    tab_s[QDEC_F] = jnp.exp((t + 1.0) * lf)
    tab_s[QDEC_B] = jnp.exp((CHUNK - t) * lb)
    tab_s[DMAT0] = jnp.exp(jnp.where(diff >= 0, diff * lf0, -diff * lb0))
    tab_s[DMAT0 + 1] = jnp.exp(jnp.where(diff >= 0, diff * lf1, -diff * lb1))

    def chunk_rows(n):
        start = n * CHUNK if isinstance(n, int) else pl.multiple_of(n * CHUNK, CHUNK)
        return pl.ds(start, CHUNK)

    def kv_body(n, carry):
        rows = chunk_rows(n)
        kf32 = k_ref[0, rows, :].astype(F32)
        kd = jnp.concatenate([(kf32 * tab_s[KDEC_F]).astype(BF16),
                              (kf32 * tab_s[KDEC_B]).astype(BF16)], axis=1)
        kv = lax.dot_general(kd, v_ref[0, rows, :], (((0,), (0,)), ((), ())),
                             preferred_element_type=F32)
        for slot in range(4):
            h = slot % 2
            kv_s[n, slot] = kv[slot * HEAD_DIM:(slot + 1) * HEAD_DIM, h * LANES:(h + 1) * LANES]
        return carry

    for n in range(N_CHUNKS):
        kv_body(n, 0)

    chunk_decay = [jnp.exp(jnp.full((1, LANES), CHUNK * lg, F32)) for lg in (lf0, lf1, lb0, lb1)]

    def scan_body(n, states):
        nb = N_CHUNKS - 1 - n
        new_states = []
        for slot, state in enumerate(states):
            at = n if slot < 2 else nb
            st_ref[at, slot] = state.astype(BF16)
            new_states.append(chunk_decay[slot] * state + kv_s[at, slot])
        return tuple(new_states)

    states = (jnp.zeros((HEAD_DIM, LANES), F32),) * 4
    for n in range(N_CHUNKS):
        states = scan_body(n, states)

    head0 = lax.broadcasted_iota(jnp.int32, (CHUNK, LANES), 1) < HEAD_DIM
    zero_v = jnp.zeros((CHUNK, LANES), BF16)
    zero_s = jnp.zeros((HEAD_DIM, LANES), BF16)

    def product_stage(chunks):
        for n in chunks:
            rows = chunk_rows(n)
            q = q_ref[0, rows, :]
            zq = jnp.zeros_like(q)
            qs = jnp.concatenate([jnp.where(head0, q, zq), jnp.where(head0, zq, q)], axis=0)
            s_s[n] = lax.dot_general(qs, k_ref[0, rows, :], (((1,), (1,)), ((), ())),
                                     preferred_element_type=F32)

    def decay_stage(chunks):
        for n in chunks:
            qf32 = q_ref[0, chunk_rows(n), :].astype(F32)
            lhs_s[n, :, 0:LANES] = (s_s[n, 0:CHUNK, :] * tab_s[DMAT0]).astype(BF16)
            lhs_s[n, :, LANES:2 * LANES] = (s_s[n, CHUNK:, :] * tab_s[DMAT0 + 1]).astype(BF16)
            lhs_s[n, :, 2 * LANES:3 * LANES] = (qf32 * tab_s[QDEC_F]).astype(BF16)
            lhs_s[n, :, 3 * LANES:] = (qf32 * tab_s[QDEC_B]).astype(BF16)

    def output_stage(chunks):
        for n in chunks:
            rows = chunk_rows(n)
            v = v_ref[0, rows, :]
            left = jnp.concatenate([v[:, :LANES], zero_v, st_ref[n, 0], zero_s, st_ref[n, 2], zero_s], axis=0)
            right = jnp.concatenate([zero_v, v[:, LANES:], zero_s, st_ref[n, 1], zero_s, st_ref[n, 3]], axis=0)
            y = jnp.dot(lhs_s[n], jnp.concatenate([left, right], axis=1),
                        preferred_element_type=F32)
            for h in range(2):
                cols = slice(h * LANES, (h + 1) * LANES)
                r = y[:, cols]
                r = r * lax.rsqrt(jnp.mean(r * r, axis=-1, keepdims=True) + EPS)
                o_ref[0, rows, cols] = (r * g_ref[0, rows, cols].astype(F32)).astype(BF16)

    first, second = range(N_CHUNKS // 2), range(N_CHUNKS // 2, N_CHUNKS)
    product_stage(first)
    decay_stage(first)
    product_stage(second)
    output_stage(first)
    decay_stage(second)
    output_stage(second)


def _retention(proj, lg_f, lg_b):
    B = proj.shape[0]
    smem = pl.BlockSpec(memory_space=pltpu.SMEM)
    return pl.pallas_call(
        _ret_kernel,
        grid=(B, R_HEADS // 2),
        in_specs=[
            smem, smem,
            pl.BlockSpec((1, SEQ, LANES), lambda b, g: (b, 0, COL_QB // LANES + g)),
            pl.BlockSpec((1, SEQ, LANES), lambda b, g: (b, 0, COL_KB // LANES + g)),
            pl.BlockSpec((1, SEQ, 2 * LANES), lambda b, g: (b, 0, COL_VB // (2 * LANES) + g)),
            pl.BlockSpec((1, SEQ, 2 * LANES), lambda b, g: (b, 0, COL_GB // (2 * LANES) + g)),
        ],
        out_specs=pl.BlockSpec((1, SEQ, 2 * LANES), lambda b, g: (b, 0, g)),
        out_shape=jax.ShapeDtypeStruct((B, SEQ, R_WIDTH), BF16),
        scratch_shapes=[pltpu.VMEM((6, CHUNK, LANES), F32),
                        pltpu.VMEM((N_CHUNKS, 4, HEAD_DIM, LANES), F32),
                        pltpu.VMEM((N_CHUNKS, 4, HEAD_DIM, LANES), BF16),
                        pltpu.VMEM((N_CHUNKS, 2 * CHUNK, LANES), F32),
                        pltpu.VMEM((N_CHUNKS, CHUNK, 4 * LANES), BF16)],
        compiler_params=pltpu.CompilerParams(
            dimension_semantics=("parallel", "parallel"), vmem_limit_bytes=VMEM_LIMIT),
        name="retention",
    )(lg_f, lg_b, proj, proj, proj, proj)


def _outproj_kernel(ya_ref, yr_ref, x_ref, gate_ref, w_ref, gfin_ref, o_ref, *, final):
    mix = jnp.concatenate([ya_ref[0], yr_ref[0]], axis=1)
    out = jnp.dot(mix, w_ref[...], preferred_element_type=F32)
    xn = x_ref[0] + gate_ref[0] * out
    if final:
        ms = jnp.mean(xn * xn, axis=-1, keepdims=True)
        xn = xn * lax.rsqrt(ms + EPS) * gfin_ref[...]
    o_ref[0] = xn


def _out_projection(ya, yr, x, mod3, w_bf16, g_final, final):
    B = x.shape[0]
    tm = OUT_ROW_TILE
    return pl.pallas_call(
        functools.partial(_outproj_kernel, final=final),
        grid=(B, SEQ // tm),
        in_specs=[
            pl.BlockSpec((1, tm, A_WIDTH), lambda b, i: (b, i, 0)),
            pl.BlockSpec((1, tm, R_WIDTH), lambda b, i: (b, i, 0)),
            pl.BlockSpec((1, tm, D_MODEL), lambda b, i: (b, i, 0)),
            pl.BlockSpec((1, 1, D_MODEL), lambda b, i: (b, 0, 2)),
            pl.BlockSpec((MIX_WIDTH, D_MODEL), lambda b, i: (0, 0)),
            pl.BlockSpec((1, D_MODEL), lambda b, i: (0, 0)),
        ],
        out_specs=pl.BlockSpec((1, tm, D_MODEL), lambda b, i: (b, i, 0)),
        out_shape=jax.ShapeDtypeStruct((B, SEQ, D_MODEL), F32),
        compiler_params=pltpu.CompilerParams(
            dimension_semantics=("parallel", "parallel"), vmem_limit_bytes=VMEM_LIMIT),
        name="out_proj",
    )(ya, yr, x, mod3, w_bf16, g_final)


def _rope_tables():
    inv = ROPE_THETA ** (-jnp.arange(0, HEAD_DIM, 2, dtype=F32) / HEAD_DIM)
    ang = jnp.arange(SEQ, dtype=F32)[:, None] * inv[None, :]
    cos, sin = jnp.cos(ang), jnp.sin(ang)
    return jnp.tile(cos, (1, 4)), jnp.concatenate([-sin, sin, -sin, sin], axis=1)


def _trunk(x, mod, g_norm, w_in_bf16, w_out_bf16, lg_f, lg_b, g_final, tables):
    B = x.shape[0]
    cos_t, sin_t = tables
    for l in range(DEPTH):
        mod3 = mod[l].reshape(B, 1, 3 * D_MODEL)
        proj = _in_projection(x, mod3, g_norm[l].reshape(1, D_MODEL), w_in_bf16[l], cos_t, sin_t)
        ya = _attention(proj)
        yr = _retention(proj, lg_f[l], lg_b[l])
        x = _out_projection(ya, yr, x, mod3, w_out_bf16[l], g_final.reshape(1, D_MODEL),
                            final=(l == DEPTH - 1))
    return x


def kernel(x_prompt, x_sample, c_prompt, c_sample, g_norm, w_ada, b_ada, w_in, w_out,
           decay_fwd, decay_bwd, g_final):
    w_in_bf16 = w_in.astype(BF16)
    w_out_bf16 = w_out.astype(BF16)
    lg_f = jax.nn.log_sigmoid(decay_fwd.astype(F32))
    lg_b = jax.nn.log_sigmoid(decay_bwd.astype(F32))
    tables = _rope_tables()
    n_prompt = c_prompt.shape[0]
    mod = _modulation(jnp.concatenate([c_prompt, c_sample], axis=0), w_ada, b_ada)
    args = (g_norm, w_in_bf16, w_out_bf16, lg_f, lg_b, g_final, tables)
    return (_trunk(x_prompt, mod[:, :n_prompt], *args), _trunk(x_sample, mod[:, n_prompt:], *args))
```

```python
import math

import jax
import jax.numpy as jnp
from jax import lax
from jax.experimental import pallas as pl
from jax.experimental.pallas import tpu as pltpu

F32 = jnp.float32
BF16 = jnp.bfloat16

D_MODEL = 1024
SEQ = 2048
DEPTH = 2
HEAD_DIM = 64
A_WIDTH = 512
R_HEADS = 4
R_WIDTH = 512
MIX_WIDTH = 1024
IN_WIDTH = 3584
CHUNK = 128
N_CHUNKS = SEQ // CHUNK
ROPE_THETA = 10000.0
EPS = 1e-6
NEG = -1e30
RADIUS = 64
LOG2E = math.log2(math.e)

LANES = 128
TQ = 128
CLASS16_PITCH = 24
ROW_TILE = 512
OUT_ROW_TILE = 1024
COL_CHUNK = 512
VMEM_LIMIT = 48 * 1024 * 1024

COL_QA, COL_KA, COL_VA, COL_GA = 0, 512, 1024, 1536
COL_QB, COL_KB, COL_VB, COL_GB = 2048, 2304, 2560, 3072


def _sigmoid(x):
    return 1.0 / (1.0 + jnp.exp(-x))


def _mod_kernel(c_ref, w_ref, b_ref, o_ref):
    c = c_ref[...]
    a = c * _sigmoid(c)
    o_ref[0] = jnp.dot(a, w_ref[0], preferred_element_type=F32,
                       precision=lax.Precision.HIGHEST) + b_ref[0]


def _modulation(c, w_ada, b_ada):
    B = c.shape[0]
    nt = 512
    return pl.pallas_call(
        _mod_kernel,
        grid=(DEPTH, 3 * D_MODEL // nt),
        in_specs=[
            pl.BlockSpec((B, D_MODEL), lambda l, j: (0, 0)),
            pl.BlockSpec((1, D_MODEL, nt), lambda l, j: (l, 0, j)),
            pl.BlockSpec((1, 1, nt), lambda l, j: (l, 0, j)),
        ],
        out_specs=pl.BlockSpec((1, B, nt), lambda l, j: (l, 0, j)),
        out_shape=jax.ShapeDtypeStruct((DEPTH, B, 3 * D_MODEL), F32),
        name="adaln_mod",
    )(c, w_ada, b_ada.reshape(DEPTH, 1, 3 * D_MODEL))


def _inproj_kernel(x_ref, shift_ref, scale_ref, g_ref, w_ref, cos_ref, sin_ref, o_ref):
    _project(x_ref[0], shift_ref, scale_ref, g_ref, w_ref, cos_ref, sin_ref, o_ref)


def _project(x, shift_ref, scale_ref, g_ref, w_ref, cos_ref, sin_ref, o_ref):
    ms = jnp.mean(x * x, axis=-1, keepdims=True)
    h = x * lax.rsqrt(ms + EPS) * g_ref[...]
    h = h * (1.0 + scale_ref[0]) + shift_ref[0]
    hb = h.astype(BF16)
    cos = cos_ref[...]
    sin = sin_ref[...]
    lane = lax.broadcasted_iota(jnp.int32, cos.shape, 1)
    first_half = (lane & 32) == 0
    for c in range(IN_WIDTH // COL_CHUNK):
        acc = jnp.dot(hb, w_ref[:, c * COL_CHUNK:(c + 1) * COL_CHUNK],
                      preferred_element_type=F32)
        for j in range(COL_CHUNK // LANES):
            col = c * COL_CHUNK + j * LANES
            a = acc[:, j * LANES:(j + 1) * LANES]
            if col < COL_VA or COL_QB <= col < COL_VB:
                partner = jnp.where(first_half, pltpu.roll(a, 96, 1), pltpu.roll(a, 32, 1))
                a = a * cos + partner * sin
            if col < COL_KA:
                a = a * (HEAD_DIM ** -0.5 * LOG2E)
            if COL_KB <= col < COL_VB:
                a = a * (HEAD_DIM ** -0.5)
            if COL_GA <= col < COL_QB or col >= COL_GB:
                a = a * _sigmoid(a)
            o_ref[0, :, col:col + LANES] = a.astype(BF16)


def _in_projection(x, mod3, g, w_bf16, cos_t, sin_t):
    B = x.shape[0]
    tm = ROW_TILE
    return pl.pallas_call(
        _inproj_kernel,
        grid=(B, SEQ // tm),
        in_specs=[
            pl.BlockSpec((1, tm, D_MODEL), lambda b, i: (b, i, 0)),
            pl.BlockSpec((1, 1, D_MODEL), lambda b, i: (b, 0, 0)),
            pl.BlockSpec((1, 1, D_MODEL), lambda b, i: (b, 0, 1)),
            pl.BlockSpec((1, D_MODEL), lambda b, i: (0, 0)),
            pl.BlockSpec((D_MODEL, IN_WIDTH), lambda b, i: (0, 0)),
            pl.BlockSpec((tm, LANES), lambda b, i: (i, 0)),
            pl.BlockSpec((tm, LANES), lambda b, i: (i, 0)),
        ],
        out_specs=pl.BlockSpec((1, tm, IN_WIDTH), lambda b, i: (b, i, 0)),
        out_shape=jax.ShapeDtypeStruct((B, SEQ, IN_WIDTH), BF16),
        compiler_params=pltpu.CompilerParams(
            dimension_semantics=("parallel", "parallel"), vmem_limit_bytes=VMEM_LIMIT),
        name="in_proj",
    )(x, mod3, mod3, g, w_bf16, cos_t, sin_t)


def _band_bias(off, width):
    r = lax.broadcasted_iota(jnp.int32, (TQ, width), 0)
    c = lax.broadcasted_iota(jnp.int32, (TQ, width), 1)
    d = r + off - c
    return jnp.where((d <= RADIUS) & (d >= -RADIUS), 0.0, NEG).astype(F32)


def _head_ones(width):
    r = lax.broadcasted_iota(jnp.int32, (2 * width, LANES), 0)
    c = lax.broadcasted_iota(jnp.int32, (2 * width, LANES), 1)
    return jnp.where((r < width) == (c < HEAD_DIM), 1.0, 0.0).astype(BF16)


def _attn_kernel(q_ref, k_ref, v_ref, g_ref, o_ref,
                 qf, kf, vf, qf4, kf4, vf4, q4, k4, v4, q16, k16, v16,
                 bias_s, biasn_s, ones_s, m4, acc4, l4, m16, acc16, l16):
    qn, kn, vn = q_ref.at[0], k_ref.at[0], v_ref.at[0]
    wide = 2 * TQ
    planes4, planes16 = (m4, acc4, l4), (m16, acc16, l16)

    def attend(b):
        width = b["width"]
        q = b["q"][pl.ds(b["q0"], TQ), :]
        head0 = lax.broadcasted_iota(jnp.int32, q.shape, 1) < HEAD_DIM
        zq = jnp.zeros_like(q)
        qs = jnp.concatenate([jnp.where(head0, q, zq), jnp.where(head0, zq, q)], axis=0)
        kw = b["k"][pl.ds(b["w0"], width), :]
        s = lax.dot_general(qs, kw, (((1,), (1,)), ((), ())), preferred_element_type=F32)
        bias = b["bias"]()
        s = s + jnp.concatenate([bias, bias], axis=0)
        m = jnp.max(s, axis=-1, keepdims=True)
        p = jnp.exp2(s - m).astype(BF16)
        pc = jnp.concatenate([p[:TQ], p[TQ:]], axis=1)
        vw = b["v"][pl.ds(b["w0"], width), :]
        vhead0 = lax.broadcasted_iota(jnp.int32, vw.shape, 1) < HEAD_DIM
        zv = jnp.zeros_like(vw)
        vs = jnp.concatenate([jnp.where(vhead0, vw, zv), jnp.where(vhead0, zv, vw)], axis=0)
        r = jnp.dot(pc, jnp.concatenate([vs, b["ones"][...]], axis=1),
                    preferred_element_type=F32)
        h0f = lax.broadcasted_iota(jnp.int32, (TQ, LANES), 1) < HEAD_DIM
        mexp = jnp.where(h0f, m[:TQ], m[TQ:])
        if b["planes"] is None:
            merge(b["block"], mexp, r[:, :LANES], r[:, LANES:])
        else:
            for plane, value in zip(b["planes"], (mexp, r[:, :LANES], r[:, LANES:])):
                plane[b["rows"], :] = value

    def class16_rows(plane, i):
        return jnp.concatenate(
            [plane[pl.ds((i * (TQ // 16) + a) * CLASS16_PITCH, 16), :] for a in range(TQ // 16)], axis=0)

    def merge(i, m0, acc0, l0):
        rows = pl.ds(i * TQ, TQ)
        m1, acc1, l1 = (plane[rows, :] for plane in planes4)
        m2, acc2, l2 = (class16_rows(plane, i) for plane in planes16)
        mx = jnp.maximum(jnp.maximum(m0, m1), m2)
        e0, e1, e2 = jnp.exp2(m0 - mx), jnp.exp2(m1 - mx), jnp.exp2(m2 - mx)
        num = e0 * acc0 + e1 * acc1 + e2 * acc2
        den = e0 * l0 + e1 * l1 + e2 * l2
        o_ref[0, rows, :] = (num / den * g_ref[0, rows, :].astype(F32)).astype(BF16)

    def banded_block(q_local, base, length):
        w_local = min(max(q_local - RADIUS, 0), length - wide)
        kind = (q_local - w_local) // RADIUS
        return base + q_local, base + w_local, lambda: bias_s[kind]

    for kind in range(3):
        bias_s[kind] = _band_bias(kind * RADIUS, wide)
    biasn_s[...] = _band_bias(0, TQ)
    ones_s[...] = _head_ones(wide)
    ones_n = ones_s.at[pl.ds(TQ, wide)]

    qf[...] = qn[...].astype(F32)
    kf[...] = kn[...].astype(F32)
    vf[...] = vn[...].astype(F32)
    L4, L16 = SEQ // 4, SEQ // 16
    for src32, mid32, dst4, dst16 in ((qf, qf4, q4, q16), (kf, kf4, k4, k16), (vf, vf4, v4, v16)):
        for r in range(4):
            x = src32[pl.ds(r, L4, stride=4), :]
            mid32[pl.ds(r * L4, L4), :] = x
            dst4[pl.ds(r * L4, L4), :] = x.astype(BF16)
        for r4 in range(4):
            for a in range(4):
                dst16[pl.ds((4 * a + r4) * L16, L16), :] = (
                    mid32[pl.ds(r4 * L4 + a, L16, stride=4), :].astype(BF16))

    per_class = L4 // TQ

    for i in range(SEQ // TQ):
        r, j = i // per_class, i % per_class
        q0, w0, bias = banded_block(j * TQ, r * L4, L4)
        attend(dict(q=q4, k=k4, v=v4, q0=q0, w0=w0, bias=bias, width=wide, ones=ones_s,
                    planes=planes4, rows=pl.ds(r + 4 * j * TQ, TQ, stride=4)))
        attend(dict(q=q16, k=k16, v=v16, q0=i * TQ, w0=i * TQ, bias=lambda: biasn_s[...],
                    width=TQ, ones=ones_n, planes=planes16, rows=pl.ds(i, TQ, stride=CLASS16_PITCH)))
    for i in range(SEQ // TQ):
        q0, w0, bias = banded_block(i * TQ, 0, SEQ)
        attend(dict(q=qn, k=kn, v=vn, q0=q0, w0=w0, bias=bias, width=wide, ones=ones_s,
                    planes=None, block=i))


def _attention(proj):
    B = proj.shape[0]
    nb = A_WIDTH // LANES

    def col(base):
        return pl.BlockSpec((1, SEQ, LANES), lambda b, g: (b, 0, base // LANES + g))

    return pl.pallas_call(
        _attn_kernel,
        grid=(B, nb),
        in_specs=[col(COL_QA), col(COL_KA), col(COL_VA), col(COL_GA)],
        out_specs=pl.BlockSpec((1, SEQ, LANES), lambda b, g: (b, 0, g)),
        out_shape=jax.ShapeDtypeStruct((B, SEQ, A_WIDTH), BF16),
        scratch_shapes=[pltpu.VMEM((SEQ, LANES), F32)] * 6
        + [pltpu.VMEM((SEQ, LANES), BF16)] * 6
        + [pltpu.VMEM((3, TQ, 2 * TQ), F32), pltpu.VMEM((TQ, TQ), F32),
           pltpu.VMEM((4 * TQ, LANES), BF16)]
        + [pltpu.VMEM((SEQ, LANES), F32)] * 3
        + [pltpu.VMEM((SEQ // 16 * CLASS16_PITCH, LANES), F32)] * 3,
        compiler_params=pltpu.CompilerParams(
            dimension_semantics=("parallel", "parallel"), vmem_limit_bytes=VMEM_LIMIT),
        name="dilated_attn",
    )(proj, proj, proj, proj)


KDEC_F, KDEC_B, QDEC_F, QDEC_B, DMAT0 = 0, 1, 2, 3, 4


def _ret_kernel(lgf_ref, lgb_ref, q_ref, k_ref, v_ref, g_ref, o_ref,
                tab_s, kv_s, st_ref, s_s, lhs_s):
    gp = pl.program_id(1)
    lf0, lf1 = lgf_ref[2 * gp], lgf_ref[2 * gp + 1]
    lb0, lb1 = lgb_ref[2 * gp], lgb_ref[2 * gp + 1]

    row = lax.broadcasted_iota(jnp.int32, (CHUNK, LANES), 0)
    lane = lax.broadcasted_iota(jnp.int32, (CHUNK, LANES), 1)
    t = row.astype(F32)
    lf = jnp.where(lane < HEAD_DIM, lf0, lf1)
    lb = jnp.where(lane < HEAD_DIM, lb0, lb1)
    diff = (row - lane).astype(F32)
    tab_s[KDEC_F] = jnp.exp((CHUNK - 1.0 - t) * lf)
    tab_s[KDEC_B] = jnp.exp(t * lb)
    tab_s[QDEC_F] = jnp.exp((t + 1.0) * lf)
    tab_s[QDEC_B] = jnp.exp((CHUNK - t) * lb)
    tab_s[DMAT0] = jnp.exp(jnp.where(diff >= 0, diff * lf0, -diff * lb0))
    tab_s[DMAT0 + 1] = jnp.exp(jnp.where(diff >= 0, diff * lf1, -diff * lb1))

    def chunk_rows(n):
        start = n * CHUNK if isinstance(n, int) else pl.multiple_of(n * CHUNK, CHUNK)
        return pl.ds(start, CHUNK)

    def kv_body(n, carry):
        rows = chunk_rows(n)
        kf32 = k_ref[0, rows, :].astype(F32)
        kd = jnp.concatenate([(kf32 * tab_s[KDEC_F]).astype(BF16),
                              (kf32 * tab_s[KDEC_B]).astype(BF16)], axis=1)
        kv = lax.dot_general(kd, v_ref[0, rows, :], (((0,), (0,)), ((), ())),
                             preferred_element_type=F32)
        for slot in range(4):
            h = slot % 2
            kv_s[n, slot] = kv[slot * HEAD_DIM:(slot + 1) * HEAD_DIM, h * LANES:(h + 1) * LANES]
        return carry

    for n in range(N_CHUNKS):
        kv_body(n, 0)

    chunk_decay = [jnp.exp(jnp.full((1, LANES), CHUNK * lg, F32)) for lg in (lf0, lf1, lb0, lb1)]

    def scan_body(n, states):
        nb = N_CHUNKS - 1 - n
        new_states = []
        for slot, state in enumerate(states):
            at = n if slot < 2 else nb
            st_ref[at, slot] = state.astype(BF16)
            new_states.append(chunk_decay[slot] * state + kv_s[at, slot])
        return tuple(new_states)

    states = (jnp.zeros((HEAD_DIM, LANES), F32),) * 4
    for n in range(N_CHUNKS):
        states = scan_body(n, states)

    head0 = lax.broadcasted_iota(jnp.int32, (CHUNK, LANES), 1) < HEAD_DIM
    zero_v = jnp.zeros((CHUNK, LANES), BF16)
    zero_s = jnp.zeros((HEAD_DIM, LANES), BF16)

    def product_stage(chunks):
        for n in chunks:
            rows = chunk_rows(n)
            q = q_ref[0, rows, :]
            zq = jnp.zeros_like(q)
            qs = jnp.concatenate([jnp.where(head0, q, zq), jnp.where(head0, zq, q)], axis=0)
            s_s[n] = lax.dot_general(qs, k_ref[0, rows, :], (((1,), (1,)), ((), ())),
                                     preferred_element_type=F32)

    def decay_stage(chunks):
        for n in chunks:
            qf32 = q_ref[0, chunk_rows(n), :].astype(F32)
            lhs_s[n, :, 0:LANES] = (s_s[n, 0:CHUNK, :] * tab_s[DMAT0]).astype(BF16)
            lhs_s[n, :, LANES:2 * LANES] = (s_s[n, CHUNK:, :] * tab_s[DMAT0 + 1]).astype(BF16)
            lhs_s[n, :, 2 * LANES:3 * LANES] = (qf32 * tab_s[QDEC_F]).astype(BF16)
            lhs_s[n, :, 3 * LANES:] = (qf32 * tab_s[QDEC_B]).astype(BF16)

    def output_stage(chunks):
        for n in chunks:
            rows = chunk_rows(n)
            v = v_ref[0, rows, :]
            left = jnp.concatenate([v[:, :LANES], zero_v, st_ref[n, 0], zero_s, st_ref[n, 2], zero_s], axis=0)
            right = jnp.concatenate([zero_v, v[:, LANES:], zero_s, st_ref[n, 1], zero_s, st_ref[n, 3]], axis=0)
            y = jnp.dot(lhs_s[n], jnp.concatenate([left, right], axis=1),
                        preferred_element_type=F32)
            for h in range(2):
                cols = slice(h * LANES, (h + 1) * LANES)
                r = y[:, cols]
                r = r * lax.rsqrt(jnp.mean(r * r, axis=-1, keepdims=True) + EPS)
                o_ref[0, rows, cols] = (r * g_ref[0, rows, cols].astype(F32)).astype(BF16)

    first, second = range(N_CHUNKS // 2), range(N_CHUNKS // 2, N_CHUNKS)
    product_stage(first)
    decay_stage(first)
    product_stage(second)
    output_stage(first)
    decay_stage(second)
    output_stage(second)


def _retention(proj, lg_f, lg_b):
    B = proj.shape[0]
    smem = pl.BlockSpec(memory_space=pltpu.SMEM)
    return pl.pallas_call(
        _ret_kernel,
        grid=(B, R_HEADS // 2),
        in_specs=[
            smem, smem,
            pl.BlockSpec((1, SEQ, LANES), lambda b, g: (b, 0, COL_QB // LANES + g)),
            pl.BlockSpec((1, SEQ, LANES), lambda b, g: (b, 0, COL_KB // LANES + g)),
            pl.BlockSpec((1, SEQ, 2 * LANES), lambda b, g: (b, 0, COL_VB // (2 * LANES) + g)),
            pl.BlockSpec((1, SEQ, 2 * LANES), lambda b, g: (b, 0, COL_GB // (2 * LANES) + g)),
        ],
        out_specs=pl.BlockSpec((1, SEQ, 2 * LANES), lambda b, g: (b, 0, g)),
        out_shape=jax.ShapeDtypeStruct((B, SEQ, R_WIDTH), BF16),
        scratch_shapes=[pltpu.VMEM((6, CHUNK, LANES), F32),
                        pltpu.VMEM((N_CHUNKS, 4, HEAD_DIM, LANES), F32),
                        pltpu.VMEM((N_CHUNKS, 4, HEAD_DIM, LANES), BF16),
                        pltpu.VMEM((N_CHUNKS, 2 * CHUNK, LANES), F32),
                        pltpu.VMEM((N_CHUNKS, CHUNK, 4 * LANES), BF16)],
        compiler_params=pltpu.CompilerParams(
            dimension_semantics=("parallel", "parallel"), vmem_limit_bytes=VMEM_LIMIT),
        name="retention",
    )(lg_f, lg_b, proj, proj, proj, proj)


def _outproj_kernel(ya_ref, yr_ref, x_ref, gate_ref, w_ref, gfin_ref, o_ref):
    mix = jnp.concatenate([ya_ref[0], yr_ref[0]], axis=1)
    out = jnp.dot(mix, w_ref[...], preferred_element_type=F32)
    xn = x_ref[0] + gate_ref[0] * out
    ms = jnp.mean(xn * xn, axis=-1, keepdims=True)
    o_ref[0] = xn * lax.rsqrt(ms + EPS) * gfin_ref[...]


def _out_projection(ya, yr, x, mod3, w_bf16, g_final):
    B = x.shape[0]
    tm = OUT_ROW_TILE
    return pl.pallas_call(
        _outproj_kernel,
        grid=(B, SEQ // tm),
        in_specs=[
            pl.BlockSpec((1, tm, A_WIDTH), lambda b, i: (b, i, 0)),
            pl.BlockSpec((1, tm, R_WIDTH), lambda b, i: (b, i, 0)),
            pl.BlockSpec((1, tm, D_MODEL), lambda b, i: (b, i, 0)),
            pl.BlockSpec((1, 1, D_MODEL), lambda b, i: (b, 0, 2)),
            pl.BlockSpec((MIX_WIDTH, D_MODEL), lambda b, i: (0, 0)),
            pl.BlockSpec((1, D_MODEL), lambda b, i: (0, 0)),
        ],
        out_specs=pl.BlockSpec((1, tm, D_MODEL), lambda b, i: (b, i, 0)),
        out_shape=jax.ShapeDtypeStruct((B, SEQ, D_MODEL), F32),
        compiler_params=pltpu.CompilerParams(
            dimension_semantics=("parallel", "parallel"), vmem_limit_bytes=VMEM_LIMIT),
        name="out_proj",
    )(ya, yr, x, mod3, w_bf16, g_final)


def _out_in_kernel(ya_ref, yr_ref, x_ref, gate_ref, wo_ref, shift_ref, scale_ref, g_ref, wi_ref,
                   cos_ref, sin_ref, xo_ref, o_ref):
    mix = jnp.concatenate([ya_ref[0], yr_ref[0]], axis=1)
    out = jnp.dot(mix, wo_ref[...], preferred_element_type=F32)
    xn = x_ref[0] + gate_ref[0] * out
    xo_ref[0] = xn
    _project(xn, shift_ref, scale_ref, g_ref, wi_ref, cos_ref, sin_ref, o_ref)


def _out_in_projection(ya, yr, x, mod3, w_out_bf16, mod3_next, g_next, w_in_bf16, cos_t, sin_t):
    B = x.shape[0]
    tm = ROW_TILE
    return pl.pallas_call(
        _out_in_kernel,
        grid=(B, SEQ // tm),
        in_specs=[
            pl.BlockSpec((1, tm, A_WIDTH), lambda b, i: (b, i, 0)),
            pl.BlockSpec((1, tm, R_WIDTH), lambda b, i: (b, i, 0)),
            pl.BlockSpec((1, tm, D_MODEL), lambda b, i: (b, i, 0)),
            pl.BlockSpec((1, 1, D_MODEL), lambda b, i: (b, 0, 2)),
            pl.BlockSpec((MIX_WIDTH, D_MODEL), lambda b, i: (0, 0)),
            pl.BlockSpec((1, 1, D_MODEL), lambda b, i: (b, 0, 0)),
            pl.BlockSpec((1, 1, D_MODEL), lambda b, i: (b, 0, 1)),
            pl.BlockSpec((1, D_MODEL), lambda b, i: (0, 0)),
            pl.BlockSpec((D_MODEL, IN_WIDTH), lambda b, i: (0, 0)),
            pl.BlockSpec((tm, LANES), lambda b, i: (i, 0)),
            pl.BlockSpec((tm, LANES), lambda b, i: (i, 0)),
        ],
        out_specs=[
            pl.BlockSpec((1, tm, D_MODEL), lambda b, i: (b, i, 0)),
            pl.BlockSpec((1, tm, IN_WIDTH), lambda b, i: (b, i, 0)),
        ],
        out_shape=[
            jax.ShapeDtypeStruct((B, SEQ, D_MODEL), F32),
            jax.ShapeDtypeStruct((B, SEQ, IN_WIDTH), BF16),
        ],
        compiler_params=pltpu.CompilerParams(
            dimension_semantics=("parallel", "parallel"), vmem_limit_bytes=VMEM_LIMIT),
        name="out_in_proj",
    )(ya, yr, x, mod3, w_out_bf16, mod3_next, mod3_next, g_next, w_in_bf16, cos_t, sin_t)


def _rope_tables():
    inv = ROPE_THETA ** (-jnp.arange(0, HEAD_DIM, 2, dtype=F32) / HEAD_DIM)
    ang = jnp.arange(SEQ, dtype=F32)[:, None] * inv[None, :]
    cos, sin = jnp.cos(ang), jnp.sin(ang)
    return jnp.tile(cos, (1, 4)), jnp.concatenate([-sin, sin, -sin, sin], axis=1)


def _trunk(x, mod, g_norm, w_in_bf16, w_out_bf16, lg_f, lg_b, g_final, tables):
    B = x.shape[0]
    cos_t, sin_t = tables
    mod3 = [mod[l].reshape(B, 1, 3 * D_MODEL) for l in range(DEPTH)]
    g_rows = [g_norm[l].reshape(1, D_MODEL) for l in range(DEPTH)]
    proj = _in_projection(x, mod3[0], g_rows[0], w_in_bf16[0], cos_t, sin_t)
    for l in range(DEPTH):
        ya = _attention(proj)
        yr = _retention(proj, lg_f[l], lg_b[l])
        if l + 1 < DEPTH:
            x, proj = _out_in_projection(ya, yr, x, mod3[l], w_out_bf16[l], mod3[l + 1], g_rows[l + 1],
                                         w_in_bf16[l + 1], cos_t, sin_t)
        else:
            x = _out_projection(ya, yr, x, mod3[l], w_out_bf16[l], g_final.reshape(1, D_MODEL))
    return x


def kernel(x_prompt, x_sample, c_prompt, c_sample, g_norm, w_ada, b_ada, w_in, w_out,
           decay_fwd, decay_bwd, g_final):
    w_in_bf16 = w_in.astype(BF16)
    w_out_bf16 = w_out.astype(BF16)
    lg_f = jax.nn.log_sigmoid(decay_fwd.astype(F32))
    lg_b = jax.nn.log_sigmoid(decay_bwd.astype(F32))
    tables = _rope_tables()
    n_prompt = c_prompt.shape[0]
    mod = _modulation(jnp.concatenate([c_prompt, c_sample], axis=0), w_ada, b_ada)
    args = (g_norm, w_in_bf16, w_out_bf16, lg_f, lg_b, g_final, tables)
    return (_trunk(x_prompt, mod[:, :n_prompt], *args), _trunk(x_sample, mod[:, n_prompt:], *args))
```

```python
import math

import jax
import jax.numpy as jnp
from jax import lax
from jax.experimental import pallas as pl
from jax.experimental.pallas import tpu as pltpu

F32 = jnp.float32
BF16 = jnp.bfloat16

D_MODEL = 1024
SEQ = 2048
DEPTH = 2
HEAD_DIM = 64
A_WIDTH = 512
R_HEADS = 4
R_WIDTH = 512
MIX_WIDTH = 1024
IN_WIDTH = 3584
CHUNK = 128
N_CHUNKS = SEQ // CHUNK
ROPE_THETA = 10000.0
EPS = 1e-6
NEG = -1e30
RADIUS = 64
LOG2E = math.log2(math.e)

LANES = 128
TQ = 128
CLASS16_PITCH = 24
ROW_TILE = 512
OUT_ROW_TILE = 1024
COL_CHUNK = 512
VMEM_LIMIT = 48 * 1024 * 1024

COL_QA, COL_KA, COL_VA, COL_GA = 0, 512, 1024, 1536
COL_QB, COL_KB, COL_VB, COL_GB = 2048, 2304, 2560, 3072
QKV_A_WIDTH = COL_GA
REST_WIDTH = IN_WIDTH - QKV_A_WIDTH


def _sigmoid(x):
    return 1.0 / (1.0 + jnp.exp(-x))


def _mod_kernel(c_ref, w_ref, b_ref, o_ref):
    c = c_ref[...]
    a = c * _sigmoid(c)
    o_ref[0] = jnp.dot(a, w_ref[0], preferred_element_type=F32,
                       precision=lax.Precision.HIGHEST) + b_ref[0]


def _modulation(c, w_ada, b_ada):
    B = c.shape[0]
    nt = 512
    return pl.pallas_call(
        _mod_kernel,
        grid=(DEPTH, 3 * D_MODEL // nt),
        in_specs=[
            pl.BlockSpec((B, D_MODEL), lambda l, j: (0, 0)),
            pl.BlockSpec((1, D_MODEL, nt), lambda l, j: (l, 0, j)),
            pl.BlockSpec((1, 1, nt), lambda l, j: (l, 0, j)),
        ],
        out_specs=pl.BlockSpec((1, B, nt), lambda l, j: (l, 0, j)),
        out_shape=jax.ShapeDtypeStruct((DEPTH, B, 3 * D_MODEL), F32),
        name="adaln_mod",
    )(c, w_ada, b_ada.reshape(DEPTH, 1, 3 * D_MODEL))


def _inproj_kernel(x_ref, shift_ref, scale_ref, g_ref, w_ref, cos_ref, sin_ref, oa_ref, o_ref):
    _project(x_ref[0], shift_ref, scale_ref, g_ref, w_ref, cos_ref, sin_ref, oa_ref, o_ref)


def _project(x, shift_ref, scale_ref, g_ref, w_ref, cos_ref, sin_ref, oa_ref, o_ref):
    ms = jnp.mean(x * x, axis=-1, keepdims=True)
    h = x * lax.rsqrt(ms + EPS) * g_ref[...]
    h = h * (1.0 + scale_ref[0]) + shift_ref[0]
    hb = h.astype(BF16)
    cos = cos_ref[...]
    sin = sin_ref[...]
    lane = lax.broadcasted_iota(jnp.int32, cos.shape, 1)
    first_half = (lane & 32) == 0
    for c in range(IN_WIDTH // COL_CHUNK):
        acc = jnp.dot(hb, w_ref[:, c * COL_CHUNK:(c + 1) * COL_CHUNK],
                      preferred_element_type=F32)
        for j in range(COL_CHUNK // LANES):
            col = c * COL_CHUNK + j * LANES
            a = acc[:, j * LANES:(j + 1) * LANES]
            if col < COL_VA or COL_QB <= col < COL_VB:
                partner = jnp.where(first_half, pltpu.roll(a, 96, 1), pltpu.roll(a, 32, 1))
                a = a * cos + partner * sin
            if col < COL_KA:
                a = a * (HEAD_DIM ** -0.5 * LOG2E)
            if COL_KB <= col < COL_VB:
                a = a * (HEAD_DIM ** -0.5)
            if COL_GA <= col < COL_QB or col >= COL_GB:
                a = a * _sigmoid(a)
            if col < QKV_A_WIDTH:
                oa_ref[col // LANES, 0] = a.astype(BF16)
            else:
                o_ref[0, :, col - QKV_A_WIDTH:col - QKV_A_WIDTH + LANES] = a.astype(BF16)


def _proj_out_specs(tm):
    return [pl.BlockSpec((QKV_A_WIDTH // LANES, 1, tm, LANES), lambda b, i: (0, b, i, 0)),
            pl.BlockSpec((1, tm, REST_WIDTH), lambda b, i: (b, i, 0))]


def _proj_out_shapes(B):
    return [jax.ShapeDtypeStruct((QKV_A_WIDTH // LANES, B, SEQ, LANES), BF16),
            jax.ShapeDtypeStruct((B, SEQ, REST_WIDTH), BF16)]


def _in_projection(x, mod3, g, w_bf16, cos_t, sin_t):
    B = x.shape[0]
    tm = ROW_TILE
    return pl.pallas_call(
        _inproj_kernel,
        grid=(B, SEQ // tm),
        in_specs=[
            pl.BlockSpec((1, tm, D_MODEL), lambda b, i: (b, i, 0)),
            pl.BlockSpec((1, 1, D_MODEL), lambda b, i: (b, 0, 0)),
            pl.BlockSpec((1, 1, D_MODEL), lambda b, i: (b, 0, 1)),
            pl.BlockSpec((1, D_MODEL), lambda b, i: (0, 0)),
            pl.BlockSpec((D_MODEL, IN_WIDTH), lambda b, i: (0, 0)),
            pl.BlockSpec((tm, LANES), lambda b, i: (i, 0)),
            pl.BlockSpec((tm, LANES), lambda b, i: (i, 0)),
        ],
        out_specs=_proj_out_specs(tm),
        out_shape=_proj_out_shapes(B),
        compiler_params=pltpu.CompilerParams(
            dimension_semantics=("parallel", "parallel"), vmem_limit_bytes=VMEM_LIMIT),
        name="in_proj",
    )(x, mod3, mod3, g, w_bf16, cos_t, sin_t)


def _band_bias(off, width):
    r = lax.broadcasted_iota(jnp.int32, (TQ, width), 0)
    c = lax.broadcasted_iota(jnp.int32, (TQ, width), 1)
    d = r + off - c
    return jnp.where((d <= RADIUS) & (d >= -RADIUS), 0.0, NEG).astype(F32)


def _head_ones(width):
    r = lax.broadcasted_iota(jnp.int32, (2 * width, LANES), 0)
    c = lax.broadcasted_iota(jnp.int32, (2 * width, LANES), 1)
    return jnp.where((r < width) == (c < HEAD_DIM), 1.0, 0.0).astype(BF16)


def _attn_kernel(q_ref, k_ref, v_ref, q4_ref, k4_ref, v4_ref, q16_ref, k16_ref, v16_ref, g_ref, o_ref,
                 bias_s, biasn_s, ones_s, m4, acc4, l4, m16, acc16, l16):
    qn, kn, vn = q_ref.at[0, 0], k_ref.at[0, 0], v_ref.at[0, 0]
    q4, k4, v4 = q4_ref.at[0, 0], k4_ref.at[0, 0], v4_ref.at[0, 0]
    q16, k16, v16 = q16_ref.at[0, 0], k16_ref.at[0, 0], v16_ref.at[0, 0]
    wide = 2 * TQ
    planes4, planes16 = (m4, acc4, l4), (m16, acc16, l16)

    def attend(b):
        width = b["width"]
        q = b["q"][pl.ds(b["q0"], TQ), b["lanes"]]
        head0 = lax.broadcasted_iota(jnp.int32, q.shape, 1) < HEAD_DIM
        zq = jnp.zeros_like(q)
        qs = jnp.concatenate([jnp.where(head0, q, zq), jnp.where(head0, zq, q)], axis=0)
        kw = b["k"][pl.ds(b["w0"], width), b["lanes"]]
        s = lax.dot_general(qs, kw, (((1,), (1,)), ((), ())), preferred_element_type=F32)
        bias = b["bias"]()
        s = s + jnp.concatenate([bias, bias], axis=0)
        m = jnp.max(s, axis=-1, keepdims=True)
        p = jnp.exp2(s - m).astype(BF16)
        pc = jnp.concatenate([p[:TQ], p[TQ:]], axis=1)
        vw = b["v"][pl.ds(b["w0"], width), b["lanes"]]
        vhead0 = lax.broadcasted_iota(jnp.int32, vw.shape, 1) < HEAD_DIM
        zv = jnp.zeros_like(vw)
        vs = jnp.concatenate([jnp.where(vhead0, vw, zv), jnp.where(vhead0, zv, vw)], axis=0)
        r = jnp.dot(pc, jnp.concatenate([vs, b["ones"][...]], axis=1),
                    preferred_element_type=F32)
        h0f = lax.broadcasted_iota(jnp.int32, (TQ, LANES), 1) < HEAD_DIM
        mexp = jnp.where(h0f, m[:TQ], m[TQ:])
        if b["planes"] is None:
            merge(b["block"], mexp, r[:, :LANES], r[:, LANES:])
        else:
            for plane, value in zip(b["planes"], (mexp, r[:, :LANES], r[:, LANES:])):
                plane[b["rows"], :] = value

    def class16_rows(plane, i):
        return jnp.concatenate(
            [plane[pl.ds((i * (TQ // 16) + a) * CLASS16_PITCH, 16), :] for a in range(TQ // 16)], axis=0)

    def merge(i, m0, acc0, l0):
        rows = pl.ds(i * TQ, TQ)
        m1, acc1, l1 = (plane[rows, :] for plane in planes4)
        m2, acc2, l2 = (class16_rows(plane, i) for plane in planes16)
        mx = jnp.maximum(jnp.maximum(m0, m1), m2)
        e0, e1, e2 = jnp.exp2(m0 - mx), jnp.exp2(m1 - mx), jnp.exp2(m2 - mx)
        num = e0 * acc0 + e1 * acc1 + e2 * acc2
        den = e0 * l0 + e1 * l1 + e2 * l2
        o_ref[0, rows, :] = (num / den * g_ref[0, rows, :].astype(F32)).astype(BF16)

    def banded_block(q_local, base, length):
        w_local = min(max(q_local - RADIUS, 0), length - wide)
        kind = (q_local - w_local) // RADIUS
        return base + q_local, base + w_local, lambda: bias_s[kind]

    for kind in range(3):
        bias_s[kind] = _band_bias(kind * RADIUS, wide)
    biasn_s[...] = _band_bias(0, TQ)
    ones_s[...] = _head_ones(wide)
    ones_n = ones_s.at[pl.ds(TQ, wide)]

    L4 = SEQ // 4
    per_class = L4 // TQ

    for i in range(SEQ // TQ):
        r, j = i // per_class, i % per_class
        q0, w0, bias = banded_block(j * TQ, 0, L4)
        attend(dict(q=q4, k=k4, v=v4, lanes=slice(r * LANES, (r + 1) * LANES),
                    q0=q0, w0=w0, bias=bias, width=wide, ones=ones_s,
                    planes=planes4, rows=pl.ds(r + 4 * j * TQ, TQ, stride=4)))
        attend(dict(q=q16, k=k16, v=v16, lanes=slice(i * LANES, (i + 1) * LANES),
                    q0=0, w0=0, bias=lambda: biasn_s[...],
                    width=TQ, ones=ones_n, planes=planes16, rows=pl.ds(i, TQ, stride=CLASS16_PITCH)))
    for i in range(SEQ // TQ):
        q0, w0, bias = banded_block(i * TQ, 0, SEQ)
        attend(dict(q=qn, k=kn, v=vn, lanes=slice(None), q0=q0, w0=w0, bias=bias, width=wide, ones=ones_s,
                    planes=None, block=i))


def _attention(qkv, rest):
    B = rest.shape[0]
    nb = A_WIDTH // LANES
    groups = qkv.shape[0]
    qkv4 = qkv.reshape(groups, B, SEQ // 4, 4 * LANES)
    qkv16 = qkv.reshape(groups, B, SEQ // 16, 16 * LANES)

    def slab(kind, classes):
        return pl.BlockSpec((1, 1, SEQ // classes, classes * LANES), lambda b, g: (kind * nb + g, b, 0, 0))

    return pl.pallas_call(
        _attn_kernel,
        grid=(B, nb),
        in_specs=[slab(0, 1), slab(1, 1), slab(2, 1), slab(0, 4), slab(1, 4), slab(2, 4),
                  slab(0, 16), slab(1, 16), slab(2, 16),
                  pl.BlockSpec((1, SEQ, LANES), lambda b, g: (b, 0, (COL_GA - QKV_A_WIDTH) // LANES + g))],
        out_specs=pl.BlockSpec((1, SEQ, LANES), lambda b, g: (b, 0, g)),
        out_shape=jax.ShapeDtypeStruct((B, SEQ, A_WIDTH), BF16),
        scratch_shapes=[pltpu.VMEM((3, TQ, 2 * TQ), F32), pltpu.VMEM((TQ, TQ), F32),
           pltpu.VMEM((4 * TQ, LANES), BF16)]
        + [pltpu.VMEM((SEQ, LANES), F32)] * 3
        + [pltpu.VMEM((SEQ // 16 * CLASS16_PITCH, LANES), F32)] * 3,
        compiler_params=pltpu.CompilerParams(
            dimension_semantics=("parallel", "parallel"), vmem_limit_bytes=VMEM_LIMIT),
        name="dilated_attn",
    )(qkv, qkv, qkv, qkv4, qkv4, qkv4, qkv16, qkv16, qkv16, rest)


KDEC_F, KDEC_B, QDEC_F, QDEC_B, DMAT0 = 0, 1, 2, 3, 4


def _ret_kernel(lgf_ref, lgb_ref, q_ref, k_ref, v_ref, g_ref, o_ref,
                tab_s, kv_s, st_ref, s_s, lhs_s):
    gp = pl.program_id(1)
    lf0, lf1 = lgf_ref[2 * gp], lgf_ref[2 * gp + 1]
    lb0, lb1 = lgb_ref[2 * gp], lgb_ref[2 * gp + 1]

    row = lax.broadcasted_iota(jnp.int32, (CHUNK, LANES), 0)
    lane = lax.broadcasted_iota(jnp.int32, (CHUNK, LANES), 1)
    t = row.astype(F32)
    lf = jnp.where(lane < HEAD_DIM, lf0, lf1)
    lb = jnp.where(lane < HEAD_DIM, lb0, lb1)
    diff = (row - lane).astype(F32)
    tab_s[KDEC_F] = jnp.exp((CHUNK - 1.0 - t) * lf)
    tab_s[KDEC_B] = jnp.exp(t * lb)
    tab_s[QDEC_F] = jnp.exp((t + 1.0) * lf)
    tab_s[QDEC_B] = jnp.exp((CHUNK - t) * lb)
    tab_s[DMAT0] = jnp.exp(jnp.where(diff >= 0, diff * lf0, -diff * lb0))
    tab_s[DMAT0 + 1] = jnp.exp(jnp.where(diff >= 0, diff * lf1, -diff * lb1))

    def chunk_rows(n):
        start = n * CHUNK if isinstance(n, int) else pl.multiple_of(n * CHUNK, CHUNK)
        return pl.ds(start, CHUNK)

    def kv_body(n, carry):
        rows = chunk_rows(n)
        kf32 = k_ref[0, rows, :].astype(F32)
        kd = jnp.concatenate([(kf32 * tab_s[KDEC_F]).astype(BF16),
                              (kf32 * tab_s[KDEC_B]).astype(BF16)], axis=1)
        kv = lax.dot_general(kd, v_ref[0, rows, :], (((0,), (0,)), ((), ())),
                             preferred_element_type=F32)
        for slot in range(4):
            h = slot % 2
            kv_s[n, slot] = kv[slot * HEAD_DIM:(slot + 1) * HEAD_DIM, h * LANES:(h + 1) * LANES]
        return carry

    for n in range(N_CHUNKS):
        kv_body(n, 0)

    chunk_decay = [jnp.exp(jnp.full((1, LANES), CHUNK * lg, F32)) for lg in (lf0, lf1, lb0, lb1)]

    def scan_body(n, states):
        nb = N_CHUNKS - 1 - n
        new_states = []
        for slot, state in enumerate(states):
            at = n if slot < 2 else nb
            st_ref[at, slot] = state.astype(BF16)
            new_states.append(chunk_decay[slot] * state + kv_s[at, slot])
        return tuple(new_states)

    states = (jnp.zeros((HEAD_DIM, LANES), F32),) * 4
    for n in range(N_CHUNKS):
        states = scan_body(n, states)

    head0 = lax.broadcasted_iota(jnp.int32, (CHUNK, LANES), 1) < HEAD_DIM
    zero_v = jnp.zeros((CHUNK, LANES), BF16)
    zero_s = jnp.zeros((HEAD_DIM, LANES), BF16)

    def product_stage(chunks):
        for n in chunks:
            rows = chunk_rows(n)
            q = q_ref[0, rows, :]
            zq = jnp.zeros_like(q)
            qs = jnp.concatenate([jnp.where(head0, q, zq), jnp.where(head0, zq, q)], axis=0)
            s_s[n] = lax.dot_general(qs, k_ref[0, rows, :], (((1,), (1,)), ((), ())),
                                     preferred_element_type=F32)

    def decay_stage(chunks):
        for n in chunks:
            qf32 = q_ref[0, chunk_rows(n), :].astype(F32)
            lhs_s[n, :, 0:LANES] = (s_s[n, 0:CHUNK, :] * tab_s[DMAT0]).astype(BF16)
            lhs_s[n, :, LANES:2 * LANES] = (s_s[n, CHUNK:, :] * tab_s[DMAT0 + 1]).astype(BF16)
            lhs_s[n, :, 2 * LANES:3 * LANES] = (qf32 * tab_s[QDEC_F]).astype(BF16)
            lhs_s[n, :, 3 * LANES:] = (qf32 * tab_s[QDEC_B]).astype(BF16)

    def output_stage(chunks):
        for n in chunks:
            rows = chunk_rows(n)
            v = v_ref[0, rows, :]
            left = jnp.concatenate([v[:, :LANES], zero_v, st_ref[n, 0], zero_s, st_ref[n, 2], zero_s], axis=0)
            right = jnp.concatenate([zero_v, v[:, LANES:], zero_s, st_ref[n, 1], zero_s, st_ref[n, 3]], axis=0)
            y = jnp.dot(lhs_s[n], jnp.concatenate([left, right], axis=1),
                        preferred_element_type=F32)
            for h in range(2):
                cols = slice(h * LANES, (h + 1) * LANES)
                r = y[:, cols]
                r = r * lax.rsqrt(jnp.mean(r * r, axis=-1, keepdims=True) + EPS)
                o_ref[0, rows, cols] = (r * g_ref[0, rows, cols].astype(F32)).astype(BF16)

    first, second = range(N_CHUNKS // 2), range(N_CHUNKS // 2, N_CHUNKS)
    product_stage(first)
    decay_stage(first)
    product_stage(second)
    output_stage(first)
    decay_stage(second)
    output_stage(second)


def _retention(proj, lg_f, lg_b):
    B = proj.shape[0]
    smem = pl.BlockSpec(memory_space=pltpu.SMEM)
    return pl.pallas_call(
        _ret_kernel,
        grid=(B, R_HEADS // 2),
        in_specs=[
            smem, smem,
            pl.BlockSpec((1, SEQ, LANES), lambda b, g: (b, 0, (COL_QB - QKV_A_WIDTH) // LANES + g)),
            pl.BlockSpec((1, SEQ, LANES), lambda b, g: (b, 0, (COL_KB - QKV_A_WIDTH) // LANES + g)),
            pl.BlockSpec((1, SEQ, 2 * LANES), lambda b, g: (b, 0, (COL_VB - QKV_A_WIDTH) // (2 * LANES) + g)),
            pl.BlockSpec((1, SEQ, 2 * LANES), lambda b, g: (b, 0, (COL_GB - QKV_A_WIDTH) // (2 * LANES) + g)),
        ],
        out_specs=pl.BlockSpec((1, SEQ, 2 * LANES), lambda b, g: (b, 0, g)),
        out_shape=jax.ShapeDtypeStruct((B, SEQ, R_WIDTH), BF16),
        scratch_shapes=[pltpu.VMEM((6, CHUNK, LANES), F32),
                        pltpu.VMEM((N_CHUNKS, 4, HEAD_DIM, LANES), F32),
                        pltpu.VMEM((N_CHUNKS, 4, HEAD_DIM, LANES), BF16),
                        pltpu.VMEM((N_CHUNKS, 2 * CHUNK, LANES), F32),
                        pltpu.VMEM((N_CHUNKS, CHUNK, 4 * LANES), BF16)],
        compiler_params=pltpu.CompilerParams(
            dimension_semantics=("parallel", "parallel"), vmem_limit_bytes=VMEM_LIMIT),
        name="retention",
    )(lg_f, lg_b, proj, proj, proj, proj)


def _outproj_kernel(ya_ref, yr_ref, x_ref, gate_ref, w_ref, gfin_ref, o_ref):
    mix = jnp.concatenate([ya_ref[0], yr_ref[0]], axis=1)
    out = jnp.dot(mix, w_ref[...], preferred_element_type=F32)
    xn = x_ref[0] + gate_ref[0] * out
    ms = jnp.mean(xn * xn, axis=-1, keepdims=True)
    o_ref[0] = xn * lax.rsqrt(ms + EPS) * gfin_ref[...]


def _out_projection(ya, yr, x, mod3, w_bf16, g_final):
    B = x.shape[0]
    tm = OUT_ROW_TILE
    return pl.pallas_call(
        _outproj_kernel,
        grid=(B, SEQ // tm),
        in_specs=[
            pl.BlockSpec((1, tm, A_WIDTH), lambda b, i: (b, i, 0)),
            pl.BlockSpec((1, tm, R_WIDTH), lambda b, i: (b, i, 0)),
            pl.BlockSpec((1, tm, D_MODEL), lambda b, i: (b, i, 0)),
            pl.BlockSpec((1, 1, D_MODEL), lambda b, i: (b, 0, 2)),
            pl.BlockSpec((MIX_WIDTH, D_MODEL), lambda b, i: (0, 0)),
            pl.BlockSpec((1, D_MODEL), lambda b, i: (0, 0)),
        ],
        out_specs=pl.BlockSpec((1, tm, D_MODEL), lambda b, i: (b, i, 0)),
        out_shape=jax.ShapeDtypeStruct((B, SEQ, D_MODEL), F32),
        compiler_params=pltpu.CompilerParams(
            dimension_semantics=("parallel", "parallel"), vmem_limit_bytes=VMEM_LIMIT),
        name="out_proj",
    )(ya, yr, x, mod3, w_bf16, g_final)


def _out_in_kernel(ya_ref, yr_ref, x_ref, gate_ref, wo_ref, shift_ref, scale_ref, g_ref, wi_ref,
                   cos_ref, sin_ref, xo_ref, oa_ref, o_ref):
    mix = jnp.concatenate([ya_ref[0], yr_ref[0]], axis=1)
    out = jnp.dot(mix, wo_ref[...], preferred_element_type=F32)
    xn = x_ref[0] + gate_ref[0] * out
    xo_ref[0] = xn
    _project(xn, shift_ref, scale_ref, g_ref, wi_ref, cos_ref, sin_ref, oa_ref, o_ref)


def _out_in_projection(ya, yr, x, mod3, w_out_bf16, mod3_next, g_next, w_in_bf16, cos_t, sin_t):
    B = x.shape[0]
    tm = ROW_TILE
    return pl.pallas_call(
        _out_in_kernel,
        grid=(B, SEQ // tm),
        in_specs=[
            pl.BlockSpec((1, tm, A_WIDTH), lambda b, i: (b, i, 0)),
            pl.BlockSpec((1, tm, R_WIDTH), lambda b, i: (b, i, 0)),
            pl.BlockSpec((1, tm, D_MODEL), lambda b, i: (b, i, 0)),
            pl.BlockSpec((1, 1, D_MODEL), lambda b, i: (b, 0, 2)),
            pl.BlockSpec((MIX_WIDTH, D_MODEL), lambda b, i: (0, 0)),
            pl.BlockSpec((1, 1, D_MODEL), lambda b, i: (b, 0, 0)),
            pl.BlockSpec((1, 1, D_MODEL), lambda b, i: (b, 0, 1)),
            pl.BlockSpec((1, D_MODEL), lambda b, i: (0, 0)),
            pl.BlockSpec((D_MODEL, IN_WIDTH), lambda b, i: (0, 0)),
            pl.BlockSpec((tm, LANES), lambda b, i: (i, 0)),
            pl.BlockSpec((tm, LANES), lambda b, i: (i, 0)),
        ],
        out_specs=[pl.BlockSpec((1, tm, D_MODEL), lambda b, i: (b, i, 0))] + _proj_out_specs(tm),
        out_shape=[jax.ShapeDtypeStruct((B, SEQ, D_MODEL), F32)] + _proj_out_shapes(B),
        compiler_params=pltpu.CompilerParams(
            dimension_semantics=("parallel", "parallel"), vmem_limit_bytes=VMEM_LIMIT),
        name="out_in_proj",
    )(ya, yr, x, mod3, w_out_bf16, mod3_next, mod3_next, g_next, w_in_bf16, cos_t, sin_t)


def _rope_tables():
    inv = ROPE_THETA ** (-jnp.arange(0, HEAD_DIM, 2, dtype=F32) / HEAD_DIM)
    ang = jnp.arange(SEQ, dtype=F32)[:, None] * inv[None, :]
    cos, sin = jnp.cos(ang), jnp.sin(ang)
    return jnp.tile(cos, (1, 4)), jnp.concatenate([-sin, sin, -sin, sin], axis=1)


def _trunk(x, mod, g_norm, w_in_bf16, w_out_bf16, lg_f, lg_b, g_final, tables):
    B = x.shape[0]
    cos_t, sin_t = tables
    mod3 = [mod[l].reshape(B, 1, 3 * D_MODEL) for l in range(DEPTH)]
    g_rows = [g_norm[l].reshape(1, D_MODEL) for l in range(DEPTH)]
    qkv, rest = _in_projection(x, mod3[0], g_rows[0], w_in_bf16[0], cos_t, sin_t)
    for l in range(DEPTH):
        ya = _attention(qkv, rest)
        yr = _retention(rest, lg_f[l], lg_b[l])
        if l + 1 < DEPTH:
            x, qkv, rest = _out_in_projection(ya, yr, x, mod3[l], w_out_bf16[l], mod3[l + 1], g_rows[l + 1],
                                         w_in_bf16[l + 1], cos_t, sin_t)
        else:
            x = _out_projection(ya, yr, x, mod3[l], w_out_bf16[l], g_final.reshape(1, D_MODEL))
    return x


def kernel(x_prompt, x_sample, c_prompt, c_sample, g_norm, w_ada, b_ada, w_in, w_out,
           decay_fwd, decay_bwd, g_final):
    w_in_bf16 = w_in.astype(BF16)
    w_out_bf16 = w_out.astype(BF16)
    lg_f = jax.nn.log_sigmoid(decay_fwd.astype(F32))
    lg_b = jax.nn.log_sigmoid(decay_bwd.astype(F32))
    tables = _rope_tables()
    n_prompt = c_prompt.shape[0]
    mod = _modulation(jnp.concatenate([c_prompt, c_sample], axis=0), w_ada, b_ada)
    args = (g_norm, w_in_bf16, w_out_bf16, lg_f, lg_b, g_final, tables)
    return (_trunk(x_prompt, mod[:, :n_prompt], *args), _trunk(x_sample, mod[:, n_prompt:], *args))
```

```python
import math

import jax
import jax.numpy as jnp
from jax import lax
from jax.experimental import pallas as pl
from jax.experimental.pallas import tpu as pltpu

F32 = jnp.float32
BF16 = jnp.bfloat16

D_MODEL = 1024
SEQ = 2048
DEPTH = 2
HEAD_DIM = 64
A_WIDTH = 512
R_HEADS = 4
R_WIDTH = 512
MIX_WIDTH = 1024
IN_WIDTH = 3584
CHUNK = 128
N_CHUNKS = SEQ // CHUNK
ROPE_THETA = 10000.0
EPS = 1e-6
NEG = -1e30
RADIUS = 64
LOG2E = math.log2(math.e)

LANES = 128
TQ = 128
CLASS16_PITCH = 24
ROW_TILE = 512
OUT_ROW_TILE = 1024
COL_CHUNK = 512
VMEM_LIMIT = 48 * 1024 * 1024

COL_QA, COL_KA, COL_VA, COL_GA = 0, 512, 1024, 1536
COL_QB, COL_KB, COL_VB, COL_GB = 2048, 2304, 2560, 3072


def _sigmoid(x):
    return 1.0 / (1.0 + jnp.exp(-x))


def _mod_kernel(c_ref, w_ref, b_ref, o_ref):
    c = c_ref[...]
    a = c * _sigmoid(c)
    o_ref[0] = jnp.dot(a, w_ref[0], preferred_element_type=F32,
                       precision=lax.Precision.HIGHEST) + b_ref[0]


def _modulation(c, w_ada, b_ada):
    B = c.shape[0]
    nt = 512
    return pl.pallas_call(
        _mod_kernel,
        grid=(DEPTH, 3 * D_MODEL // nt),
        in_specs=[
            pl.BlockSpec((B, D_MODEL), lambda l, j: (0, 0)),
            pl.BlockSpec((1, D_MODEL, nt), lambda l, j: (l, 0, j)),
            pl.BlockSpec((1, 1, nt), lambda l, j: (l, 0, j)),
        ],
        out_specs=pl.BlockSpec((1, B, nt), lambda l, j: (l, 0, j)),
        out_shape=jax.ShapeDtypeStruct((DEPTH, B, 3 * D_MODEL), F32),
        name="adaln_mod",
    )(c, w_ada, b_ada.reshape(DEPTH, 1, 3 * D_MODEL))


def _inproj_kernel(x_ref, shift_ref, scale_ref, g_ref, w_ref, cos_ref, sin_ref, o_ref):
    _project(x_ref[0], shift_ref, scale_ref, g_ref, w_ref, cos_ref, sin_ref, o_ref)


def _project(x, shift_ref, scale_ref, g_ref, w_ref, cos_ref, sin_ref, o_ref):
    ms = jnp.mean(x * x, axis=-1, keepdims=True)
    h = x * lax.rsqrt(ms + EPS) * g_ref[...]
    h = h * (1.0 + scale_ref[0]) + shift_ref[0]
    hb = h.astype(BF16)
    cos = cos_ref[...]
    sin = sin_ref[...]
    lane = lax.broadcasted_iota(jnp.int32, cos.shape, 1)
    first_half = (lane & 32) == 0
    for c in range(IN_WIDTH // COL_CHUNK):
        acc = jnp.dot(hb, w_ref[:, c * COL_CHUNK:(c + 1) * COL_CHUNK],
                      preferred_element_type=F32)
        for j in range(COL_CHUNK // LANES):
            col = c * COL_CHUNK + j * LANES
            a = acc[:, j * LANES:(j + 1) * LANES]
            if col < COL_VA or COL_QB <= col < COL_VB:
                partner = jnp.where(first_half, pltpu.roll(a, 96, 1), pltpu.roll(a, 32, 1))
                a = a * cos + partner * sin
            if col < COL_KA:
                a = a * (HEAD_DIM ** -0.5 * LOG2E)
            if COL_KB <= col < COL_VB:
                a = a * (HEAD_DIM ** -0.5)
            if COL_GA <= col < COL_QB or col >= COL_GB:
                a = a * _sigmoid(a)
            o_ref[0, :, col:col + LANES] = a.astype(BF16)


def _in_projection(x, mod3, g, w_bf16, cos_t, sin_t):
    B = x.shape[0]
    tm = ROW_TILE
    return pl.pallas_call(
        _inproj_kernel,
        grid=(B, SEQ // tm),
        in_specs=[
            pl.BlockSpec((1, tm, D_MODEL), lambda b, i: (b, i, 0)),
            pl.BlockSpec((1, 1, D_MODEL), lambda b, i: (b, 0, 0)),
            pl.BlockSpec((1, 1, D_MODEL), lambda b, i: (b, 0, 1)),
            pl.BlockSpec((1, D_MODEL), lambda b, i: (0, 0)),
            pl.BlockSpec((D_MODEL, IN_WIDTH), lambda b, i: (0, 0)),
            pl.BlockSpec((tm, LANES), lambda b, i: (i, 0)),
            pl.BlockSpec((tm, LANES), lambda b, i: (i, 0)),
        ],
        out_specs=pl.BlockSpec((1, tm, IN_WIDTH), lambda b, i: (b, i, 0)),
        out_shape=jax.ShapeDtypeStruct((B, SEQ, IN_WIDTH), BF16),
        compiler_params=pltpu.CompilerParams(
            dimension_semantics=("parallel", "parallel"), vmem_limit_bytes=VMEM_LIMIT),
        name="in_proj",
    )(x, mod3, mod3, g, w_bf16, cos_t, sin_t)


def _band_bias(off, width):
    r = lax.broadcasted_iota(jnp.int32, (TQ, width), 0)
    c = lax.broadcasted_iota(jnp.int32, (TQ, width), 1)
    d = r + off - c
    return jnp.where((d <= RADIUS) & (d >= -RADIUS), 0.0, NEG).astype(F32)


def _head_ones(width):
    r = lax.broadcasted_iota(jnp.int32, (2 * width, LANES), 0)
    c = lax.broadcasted_iota(jnp.int32, (2 * width, LANES), 1)
    return jnp.where((r < width) == (c < HEAD_DIM), 1.0, 0.0).astype(BF16)


def _attn_kernel(q_ref, k_ref, v_ref, g_ref, o_ref,
                 qf, kf, vf, qf4, kf4, vf4, q4, k4, v4, q16, k16, v16,
                 bias_s, biasn_s, ones_s, m4, acc4, l4, m16, acc16, l16):
    qn, kn, vn = q_ref.at[0], k_ref.at[0], v_ref.at[0]
    wide = 2 * TQ
    planes4, planes16 = (m4, acc4, l4), (m16, acc16, l16)

    def attend(b):
        width = b["width"]
        q = b["q"][pl.ds(b["q0"], TQ), :]
        head0 = lax.broadcasted_iota(jnp.int32, q.shape, 1) < HEAD_DIM
        zq = jnp.zeros_like(q)
        qs = jnp.concatenate([jnp.where(head0, q, zq), jnp.where(head0, zq, q)], axis=0)
        kw = b["k"][pl.ds(b["w0"], width), :]
        s = lax.dot_general(qs, kw, (((1,), (1,)), ((), ())), preferred_element_type=F32)
        bias = b["bias"]()
        s = s + jnp.concatenate([bias, bias], axis=0)
        m = jnp.max(s, axis=-1, keepdims=True)
        p = jnp.exp2(s - m).astype(BF16)
        pc = jnp.concatenate([p[:TQ], p[TQ:]], axis=1)
        vw = b["v"][pl.ds(b["w0"], width), :]
        vhead0 = lax.broadcasted_iota(jnp.int32, vw.shape, 1) < HEAD_DIM
        zv = jnp.zeros_like(vw)
        vs = jnp.concatenate([jnp.where(vhead0, vw, zv), jnp.where(vhead0, zv, vw)], axis=0)
        r = jnp.dot(pc, jnp.concatenate([vs, b["ones"][...]], axis=1),
                    preferred_element_type=F32)
        h0f = lax.broadcasted_iota(jnp.int32, (TQ, LANES), 1) < HEAD_DIM
        mexp = jnp.where(h0f, m[:TQ], m[TQ:])
        if b["planes"] is None:
            merge(b["block"], mexp, r[:, :LANES], r[:, LANES:])
        else:
            for plane, value in zip(b["planes"], (mexp, r[:, :LANES], r[:, LANES:])):
                plane[b["rows"], :] = value

    def class16_rows(plane, i):
        return jnp.concatenate(
            [plane[pl.ds((i * (TQ // 16) + a) * CLASS16_PITCH, 16), :] for a in range(TQ // 16)], axis=0)

    def merge(i, m0, acc0, l0):
        rows = pl.ds(i * TQ, TQ)
        m1, acc1, l1 = (plane[rows, :] for plane in planes4)
        m2, acc2, l2 = (class16_rows(plane, i) for plane in planes16)
        mx = jnp.maximum(jnp.maximum(m0, m1), m2)
        e0, e1, e2 = jnp.exp2(m0 - mx), jnp.exp2(m1 - mx), jnp.exp2(m2 - mx)
        num = e0 * acc0 + e1 * acc1 + e2 * acc2
        den = e0 * l0 + e1 * l1 + e2 * l2
        o_ref[0, rows, :] = (num / den * g_ref[0, rows, :].astype(F32)).astype(BF16)

    def banded_block(q_local, base, length):
        w_local = min(max(q_local - RADIUS, 0), length - wide)
        kind = (q_local - w_local) // RADIUS
        return base + q_local, base + w_local, lambda: bias_s[kind]

    for kind in range(3):
        bias_s[kind] = _band_bias(kind * RADIUS, wide)
    biasn_s[...] = _band_bias(0, TQ)
    ones_s[...] = _head_ones(wide)
    ones_n = ones_s.at[pl.ds(TQ, wide)]

    qf[...] = qn[...].astype(F32)
    kf[...] = kn[...].astype(F32)
    vf[...] = vn[...].astype(F32)
    L4, L16 = SEQ // 4, SEQ // 16
    for src32, mid32, dst4, dst16 in ((qf, qf4, q4, q16), (kf, kf4, k4, k16), (vf, vf4, v4, v16)):
        for r in range(4):
            x = src32[pl.ds(r, L4, stride=4), :]
            mid32[pl.ds(r * L4, L4), :] = x
            dst4[pl.ds(r * L4, L4), :] = x.astype(BF16)
        for r4 in range(4):
            for a in range(4):
                dst16[pl.ds((4 * a + r4) * L16, L16), :] = (
                    mid32[pl.ds(r4 * L4 + a, L16, stride=4), :].astype(BF16))

    per_class = L4 // TQ

    for i in range(SEQ // TQ):
        r, j = i // per_class, i % per_class
        q0, w0, bias = banded_block(j * TQ, r * L4, L4)
        attend(dict(q=q4, k=k4, v=v4, q0=q0, w0=w0, bias=bias, width=wide, ones=ones_s,
                    planes=planes4, rows=pl.ds(r + 4 * j * TQ, TQ, stride=4)))
        attend(dict(q=q16, k=k16, v=v16, q0=i * TQ, w0=i * TQ, bias=lambda: biasn_s[...],
                    width=TQ, ones=ones_n, planes=planes16, rows=pl.ds(i, TQ, stride=CLASS16_PITCH)))
    for i in range(SEQ // TQ):
        q0, w0, bias = banded_block(i * TQ, 0, SEQ)
        attend(dict(q=qn, k=kn, v=vn, q0=q0, w0=w0, bias=bias, width=wide, ones=ones_s,
                    planes=None, block=i))


def _attention(proj):
    B = proj.shape[0]
    nb = A_WIDTH // LANES

    def col(base):
        return pl.BlockSpec((1, SEQ, LANES), lambda b, g: (b, 0, base // LANES + g))

    return pl.pallas_call(
        _attn_kernel,
        grid=(B, nb),
        in_specs=[col(COL_QA), col(COL_KA), col(COL_VA), col(COL_GA)],
        out_specs=pl.BlockSpec((1, SEQ, LANES), lambda b, g: (b, 0, g)),
        out_shape=jax.ShapeDtypeStruct((B, SEQ, A_WIDTH), BF16),
        scratch_shapes=[pltpu.VMEM((SEQ, LANES), F32)] * 6
        + [pltpu.VMEM((SEQ, LANES), BF16)] * 6
        + [pltpu.VMEM((3, TQ, 2 * TQ), F32), pltpu.VMEM((TQ, TQ), F32),
           pltpu.VMEM((4 * TQ, LANES), BF16)]
        + [pltpu.VMEM((SEQ, LANES), F32)] * 3
        + [pltpu.VMEM((SEQ // 16 * CLASS16_PITCH, LANES), F32)] * 3,
        compiler_params=pltpu.CompilerParams(
            dimension_semantics=("parallel", "parallel"), vmem_limit_bytes=VMEM_LIMIT),
        name="dilated_attn",
    )(proj, proj, proj, proj)


KDEC_F, KDEC_B, QDEC_F, QDEC_B, DMAT0 = 0, 1, 2, 3, 4


def _ret_kernel(lgf_ref, lgb_ref, q_ref, k_ref, v_ref, g_ref, o_ref,
                tab_s, kv_s, st_ref, s_s, lhs_s):
    gp = pl.program_id(1)
    lf0, lf1 = lgf_ref[2 * gp], lgf_ref[2 * gp + 1]
    lb0, lb1 = lgb_ref[2 * gp], lgb_ref[2 * gp + 1]

    row = lax.broadcasted_iota(jnp.int32, (CHUNK, LANES), 0)
    lane = lax.broadcasted_iota(jnp.int32, (CHUNK, LANES), 1)
    t = row.astype(F32)
    lf = jnp.where(lane < HEAD_DIM, lf0, lf1)
    lb = jnp.where(lane < HEAD_DIM, lb0, lb1)
    diff = (row - lane).astype(F32)
    tab_s[KDEC_F] = jnp.exp((CHUNK - 1.0 - t) * lf)
    tab_s[KDEC_B] = jnp.exp(t * lb)
    tab_s[QDEC_F] = jnp.exp((t + 1.0) * lf)
    tab_s[QDEC_B] = jnp.exp((CHUNK - t) * lb)
    tab_s[DMAT0] = jnp.exp(jnp.where(diff >= 0, diff * lf0, -diff * lb0))
    tab_s[DMAT0 + 1] = jnp.exp(jnp.where(diff >= 0, diff * lf1, -diff * lb1))

    def chunk_rows(n):
        return pl.ds(n * CHUNK, CHUNK)

    for n in range(N_CHUNKS):
        rows = chunk_rows(n)
        kf32 = k_ref[0, rows, :].astype(F32)
        kd = jnp.concatenate([(kf32 * tab_s[KDEC_F]).astype(BF16),
                              (kf32 * tab_s[KDEC_B]).astype(BF16)], axis=1)
        kv = lax.dot_general(kd, v_ref[0, rows, :], (((0,), (0,)), ((), ())),
                             preferred_element_type=F32)
        for slot in range(4):
            h = slot % 2
            kv_s[n, slot] = kv[slot * HEAD_DIM:(slot + 1) * HEAD_DIM, h * LANES:(h + 1) * LANES]

    chunk_decay = [jnp.exp(jnp.full((1, LANES), CHUNK * lg, F32)) for lg in (lf0, lf1, lb0, lb1)]

    def scan_body(n, states):
        nb = N_CHUNKS - 1 - n
        new_states = []
        for slot, state in enumerate(states):
            at = n if slot < 2 else nb
            st_ref[at, slot] = state.astype(BF16)
            new_states.append(chunk_decay[slot] * state + kv_s[at, slot])
        return tuple(new_states)

    states = (jnp.zeros((HEAD_DIM, LANES), F32),) * 4
    for n in range(N_CHUNKS):
        states = scan_body(n, states)

    head0 = lax.broadcasted_iota(jnp.int32, (CHUNK, LANES), 1) < HEAD_DIM
    zero_v = jnp.zeros((CHUNK, LANES), BF16)
    zero_s = jnp.zeros((HEAD_DIM, LANES), BF16)

    def product_stage(chunks):
        for n in chunks:
            rows = chunk_rows(n)
            q = q_ref[0, rows, :]
            zq = jnp.zeros_like(q)
            qs = jnp.concatenate([jnp.where(head0, q, zq), jnp.where(head0, zq, q)], axis=0)
            s_s[n] = lax.dot_general(qs, k_ref[0, rows, :], (((1,), (1,)), ((), ())),
                                     preferred_element_type=F32)

    def decay_stage(chunks):
        for n in chunks:
            qf32 = q_ref[0, chunk_rows(n), :].astype(F32)
            lhs_s[n, :, 0:LANES] = (s_s[n, 0:CHUNK, :] * tab_s[DMAT0]).astype(BF16)
            lhs_s[n, :, LANES:2 * LANES] = (s_s[n, CHUNK:, :] * tab_s[DMAT0 + 1]).astype(BF16)
            lhs_s[n, :, 2 * LANES:3 * LANES] = (qf32 * tab_s[QDEC_F]).astype(BF16)
            lhs_s[n, :, 3 * LANES:] = (qf32 * tab_s[QDEC_B]).astype(BF16)

    def output_stage(chunks):
        for n in chunks:
            rows = chunk_rows(n)
            v = v_ref[0, rows, :]
            left = jnp.concatenate([v[:, :LANES], zero_v, st_ref[n, 0], zero_s, st_ref[n, 2], zero_s], axis=0)
            right = jnp.concatenate([zero_v, v[:, LANES:], zero_s, st_ref[n, 1], zero_s, st_ref[n, 3]], axis=0)
            y = jnp.dot(lhs_s[n], jnp.concatenate([left, right], axis=1),
                        preferred_element_type=F32)
            for h in range(2):
                cols = slice(h * LANES, (h + 1) * LANES)
                r = y[:, cols]
                r = r * lax.rsqrt(jnp.mean(r * r, axis=-1, keepdims=True) + EPS)
                o_ref[0, rows, cols] = (r * g_ref[0, rows, cols].astype(F32)).astype(BF16)

    first, second = range(N_CHUNKS // 2), range(N_CHUNKS // 2, N_CHUNKS)
    product_stage(first)
    decay_stage(first)
    product_stage(second)
    output_stage(first)
    decay_stage(second)
    output_stage(second)


def _retention(proj, lg_f, lg_b):
    B = proj.shape[0]
    smem = pl.BlockSpec(memory_space=pltpu.SMEM)
    return pl.pallas_call(
        _ret_kernel,
        grid=(B, R_HEADS // 2),
        in_specs=[
            smem, smem,
            pl.BlockSpec((1, SEQ, LANES), lambda b, g: (b, 0, COL_QB // LANES + g)),
            pl.BlockSpec((1, SEQ, LANES), lambda b, g: (b, 0, COL_KB // LANES + g)),
            pl.BlockSpec((1, SEQ, 2 * LANES), lambda b, g: (b, 0, COL_VB // (2 * LANES) + g)),
            pl.BlockSpec((1, SEQ, 2 * LANES), lambda b, g: (b, 0, COL_GB // (2 * LANES) + g)),
        ],
        out_specs=pl.BlockSpec((1, SEQ, 2 * LANES), lambda b, g: (b, 0, g)),
        out_shape=jax.ShapeDtypeStruct((B, SEQ, R_WIDTH), BF16),
        scratch_shapes=[pltpu.VMEM((6, CHUNK, LANES), F32),
                        pltpu.VMEM((N_CHUNKS, 4, HEAD_DIM, LANES), F32),
                        pltpu.VMEM((N_CHUNKS, 4, HEAD_DIM, LANES), BF16),
                        pltpu.VMEM((N_CHUNKS, 2 * CHUNK, LANES), F32),
                        pltpu.VMEM((N_CHUNKS, CHUNK, 4 * LANES), BF16)],
        compiler_params=pltpu.CompilerParams(
            dimension_semantics=("parallel", "parallel"), vmem_limit_bytes=VMEM_LIMIT),
        name="retention",
    )(lg_f, lg_b, proj, proj, proj, proj)


def _outproj_kernel(ya_ref, yr_ref, x_ref, gate_ref, w_ref, gfin_ref, o_ref):
    mix = jnp.concatenate([ya_ref[0], yr_ref[0]], axis=1)
    out = jnp.dot(mix, w_ref[...], preferred_element_type=F32)
    xn = x_ref[0] + gate_ref[0] * out
    ms = jnp.mean(xn * xn, axis=-1, keepdims=True)
    o_ref[0] = xn * lax.rsqrt(ms + EPS) * gfin_ref[...]


def _out_projection(ya, yr, x, mod3, w_bf16, g_final):
    B = x.shape[0]
    tm = OUT_ROW_TILE
    return pl.pallas_call(
        _outproj_kernel,
        grid=(B, SEQ // tm),
        in_specs=[
            pl.BlockSpec((1, tm, A_WIDTH), lambda b, i: (b, i, 0)),
            pl.BlockSpec((1, tm, R_WIDTH), lambda b, i: (b, i, 0)),
            pl.BlockSpec((1, tm, D_MODEL), lambda b, i: (b, i, 0)),
            pl.BlockSpec((1, 1, D_MODEL), lambda b, i: (b, 0, 2)),
            pl.BlockSpec((MIX_WIDTH, D_MODEL), lambda b, i: (0, 0)),
            pl.BlockSpec((1, D_MODEL), lambda b, i: (0, 0)),
        ],
        out_specs=pl.BlockSpec((1, tm, D_MODEL), lambda b, i: (b, i, 0)),
        out_shape=jax.ShapeDtypeStruct((B, SEQ, D_MODEL), F32),
        compiler_params=pltpu.CompilerParams(
            dimension_semantics=("parallel", "parallel"), vmem_limit_bytes=VMEM_LIMIT),
        name="out_proj",
    )(ya, yr, x, mod3, w_bf16, g_final)


def _out_in_kernel(ya_ref, yr_ref, x_ref, gate_ref, wo_ref, shift_ref, scale_ref, g_ref, wi_ref,
                   cos_ref, sin_ref, xo_ref, o_ref):
    mix = jnp.concatenate([ya_ref[0], yr_ref[0]], axis=1)
    out = jnp.dot(mix, wo_ref[...], preferred_element_type=F32)
    xn = x_ref[0] + gate_ref[0] * out
    xo_ref[0] = xn
    _project(xn, shift_ref, scale_ref, g_ref, wi_ref, cos_ref, sin_ref, o_ref)


def _out_in_projection(ya, yr, x, mod3, w_out_bf16, mod3_next, g_next, w_in_bf16, cos_t, sin_t):
    B = x.shape[0]
    tm = ROW_TILE
    return pl.pallas_call(
        _out_in_kernel,
        grid=(B, SEQ // tm),
        in_specs=[
            pl.BlockSpec((1, tm, A_WIDTH), lambda b, i: (b, i, 0)),
            pl.BlockSpec((1, tm, R_WIDTH), lambda b, i: (b, i, 0)),
            pl.BlockSpec((1, tm, D_MODEL), lambda b, i: (b, i, 0)),
            pl.BlockSpec((1, 1, D_MODEL), lambda b, i: (b, 0, 2)),
            pl.BlockSpec((MIX_WIDTH, D_MODEL), lambda b, i: (0, 0)),
            pl.BlockSpec((1, 1, D_MODEL), lambda b, i: (b, 0, 0)),
            pl.BlockSpec((1, 1, D_MODEL), lambda b, i: (b, 0, 1)),
            pl.BlockSpec((1, D_MODEL), lambda b, i: (0, 0)),
            pl.BlockSpec((D_MODEL, IN_WIDTH), lambda b, i: (0, 0)),
            pl.BlockSpec((tm, LANES), lambda b, i: (i, 0)),
            pl.BlockSpec((tm, LANES), lambda b, i: (i, 0)),
        ],
        out_specs=[
            pl.BlockSpec((1, tm, D_MODEL), lambda b, i: (b, i, 0)),
            pl.BlockSpec((1, tm, IN_WIDTH), lambda b, i: (b, i, 0)),
        ],
        out_shape=[
            jax.ShapeDtypeStruct((B, SEQ, D_MODEL), F32),
            jax.ShapeDtypeStruct((B, SEQ, IN_WIDTH), BF16),
        ],
        compiler_params=pltpu.CompilerParams(
            dimension_semantics=("parallel", "parallel"), vmem_limit_bytes=VMEM_LIMIT),
        name="out_in_proj",
    )(ya, yr, x, mod3, w_out_bf16, mod3_next, mod3_next, g_next, w_in_bf16, cos_t, sin_t)


def _rope_tables():
    inv = ROPE_THETA ** (-jnp.arange(0, HEAD_DIM, 2, dtype=F32) / HEAD_DIM)
    ang = jnp.arange(SEQ, dtype=F32)[:, None] * inv[None, :]
    cos, sin = jnp.cos(ang), jnp.sin(ang)
    return jnp.tile(cos, (1, 4)), jnp.concatenate([-sin, sin, -sin, sin], axis=1)


def _trunk(x, mod, g_norm, w_in_bf16, w_out_bf16, lg_f, lg_b, g_final, tables):
    B = x.shape[0]
    cos_t, sin_t = tables
    mod3 = [mod[l].reshape(B, 1, 3 * D_MODEL) for l in range(DEPTH)]
    g_rows = [g_norm[l].reshape(1, D_MODEL) for l in range(DEPTH)]
    proj = _in_projection(x, mod3[0], g_rows[0], w_in_bf16[0], cos_t, sin_t)
    for l in range(DEPTH):
        ya = _attention(proj)
        yr = _retention(proj, lg_f[l], lg_b[l])
        if l + 1 < DEPTH:
            x, proj = _out_in_projection(ya, yr, x, mod3[l], w_out_bf16[l], mod3[l + 1], g_rows[l + 1],
                                         w_in_bf16[l + 1], cos_t, sin_t)
        else:
            x = _out_projection(ya, yr, x, mod3[l], w_out_bf16[l], g_final.reshape(1, D_MODEL))
    return x


def kernel(x_prompt, x_sample, c_prompt, c_sample, g_norm, w_ada, b_ada, w_in, w_out,
           decay_fwd, decay_bwd, g_final):
    w_in_bf16 = w_in.astype(BF16)
    w_out_bf16 = w_out.astype(BF16)
    lg_f = jax.nn.log_sigmoid(decay_fwd.astype(F32))
    lg_b = jax.nn.log_sigmoid(decay_bwd.astype(F32))
    tables = _rope_tables()
    n_prompt = c_prompt.shape[0]
    mod = _modulation(jnp.concatenate([c_prompt, c_sample], axis=0), w_ada, b_ada)
    args = (g_norm, w_in_bf16, w_out_bf16, lg_f, lg_b, g_final, tables)
    return (_trunk(x_prompt, mod[:, :n_prompt], *args), _trunk(x_sample, mod[:, n_prompt:], *args))
```
